```python
import jax, jax.numpy as jnp
from jax import lax
import numpy as np

D_MODEL = 4096
BATCH = 16
SEQ = 256
DEPTH = 2
DEC_BATCH = 8
DEC_SEQ = 1024
PAST_LEN = 256

GRID_W = 64
EPS = 1e-6
N_MOD = 6
N_BRANCH = 3
CONV_DIM = 2048
CONV_WIDTH = 3
N_Q_HEADS = 16
N_KV_HEADS = 4
GROUP = N_Q_HEADS // N_KV_HEADS
HEAD_DIM = 128
ATTN_DIM = N_Q_HEADS * HEAD_DIM
KV_DIM = N_KV_HEADS * HEAD_DIM
WINDOW = 128
BLOCK = 128
ROPE_BASE = 10000.0
RET_HEADS = 8
RET_DK = 256
RET_DV = 256
RET_QK_DIM = RET_HEADS * RET_DK
RET_V_DIM = RET_HEADS * RET_DV
RET_CHUNK = 128
D_FF = 4 * D_MODEL
IN_SIZES = (CONV_DIM, CONV_DIM, CONV_DIM, ATTN_DIM, KV_DIM, KV_DIM, RET_QK_DIM, RET_QK_DIM, RET_V_DIM, RET_V_DIM, RET_V_DIM, D_MODEL, D_MODEL, D_MODEL)
IN_COLS = 3 * CONV_DIM + ATTN_DIM + 2 * KV_DIM + 2 * RET_QK_DIM + 3 * RET_V_DIM + N_BRANCH * D_MODEL

kernel_name = 'bidir_hybrid_dit_prefix_step'


def rms_norm(x, g):
    xf = x.astype(jnp.float32)
    y = xf * lax.rsqrt(jnp.mean(xf * xf, axis=-1, keepdims=True) + EPS)
    return (y * g.astype(jnp.float32)).astype(x.dtype)


def adaln(cvec, p):
    mod = jax.nn.silu(cvec) @ p['w_mod'] + p['b_mod']
    return jnp.split(mod, N_MOD, axis=-1)


def modulate(x, shift, scale):
    return x * (1 + scale[:, None, :]) + shift[:, None, :]


def split_in_proj(z):
    idx = np.cumsum(IN_SIZES)[:-1].tolist()
    return jnp.split(z, idx, axis=-1)


def short_conv(u, w, b):
    up = jnp.pad(u, ((0, 0), (1, 1), (0, 0)))
    return w[0] * up[:, :-2] + w[1] * up[:, 1:-1] + w[2] * up[:, 2:] + b


def axial_rope(x):
    L, dh = x.shape[1], x.shape[-1]
    rows = L // GRID_W
    row_id = jnp.repeat(jnp.arange(rows), GRID_W).astype(jnp.float32)
    col_id = jnp.tile(jnp.arange(GRID_W), rows).astype(jnp.float32)
    half = dh // 2
    freqs = ROPE_BASE ** (-jnp.arange(0, half, 2, dtype=jnp.float32) / half)

    def rot(t, pos):
        ang = pos[:, None] * freqs[None, :]
        cos = jnp.cos(ang)[None, :, None, :].astype(t.dtype)
        sin = jnp.sin(ang)[None, :, None, :].astype(t.dtype)
        t1, t2 = jnp.split(t, 2, axis=-1)
        return jnp.concatenate([t1 * cos - t2 * sin, t2 * cos + t1 * sin], axis=-1)

    return jnp.concatenate([rot(x[..., :half], row_id), rot(x[..., half:], col_id)], axis=-1)


def sink_softmax(s, sink):
    sink = sink.astype(jnp.float32)
    m = jnp.maximum(jnp.max(s, axis=-1, keepdims=True), sink)
    e = jnp.exp(s - m)
    return e / (jnp.sum(e, axis=-1, keepdims=True) + jnp.exp(sink - m))


def context_attention(q, k, v, sink):
    B, L = q.shape[:2]
    qg = q.reshape(B, L, N_KV_HEADS, GROUP, HEAD_DIM)
    s = jnp.einsum('bqkgd,bskd->bkgqs', qg, k).astype(jnp.float32) * HEAD_DIM ** -0.5
    pr = sink_softmax(s, sink.reshape(N_KV_HEADS, GROUP, 1, 1)).astype(v.dtype)
    o = jnp.einsum('bkgqs,bskd->bqkgd', pr, v)
    return o.reshape(B, L, ATTN_DIM)


def banded_attention(q, k, v, k_ctx, v_ctx, sink):
    B, L = q.shape[:2]
    nb = L // BLOCK
    scale = HEAD_DIM ** -0.5
    qb = q.reshape(B, nb, BLOCK, N_KV_HEADS, GROUP, HEAD_DIM)

    def band(t):
        tp = jnp.pad(t, ((0, 0), (BLOCK, BLOCK), (0, 0), (0, 0))).reshape(B, nb + 2, BLOCK, N_KV_HEADS, HEAD_DIM)
        return jnp.concatenate([tp[:, :-2], tp[:, 1:-1], tp[:, 2:]], axis=2)

    kb, vb = band(k), band(v)
    qi = jnp.arange(BLOCK)[:, None]
    kj = jnp.arange(3 * BLOCK)[None, :] - BLOCK
    kpos = jnp.arange(nb)[:, None, None] * BLOCK + kj[None]
    valid = (jnp.abs(kj - qi)[None] <= WINDOW) & (kpos >= 0) & (kpos < L)
    s_loc = jnp.einsum('bnqkgd,bnskd->bnkgqs', qb, kb).astype(jnp.float32) * scale
    s_loc = jnp.where(valid[None, :, None, None], s_loc, -jnp.inf)
    s_ctx = jnp.einsum('bnqkgd,bskd->bnkgqs', qb, k_ctx).astype(jnp.float32) * scale
    pr = sink_softmax(jnp.concatenate([s_loc, s_ctx], axis=-1), sink.reshape(N_KV_HEADS, GROUP, 1, 1)).astype(v.dtype)
    p_loc, p_ctx = pr[..., :3 * BLOCK], pr[..., 3 * BLOCK:]
    o = jnp.einsum('bnkgqs,bnskd->bnqkgd', p_loc, vb) + jnp.einsum('bnkgqs,bskd->bnqkgd', p_ctx, v_ctx)
    return o.reshape(B, L, ATTN_DIM)


def retention_scan(q, k, v, log_gamma, state0):
    B, L = q.shape[:2]
    n = L // RET_CHUNK

    def chunks(t):
        return t.reshape(B, n, RET_CHUNK, *t.shape[2:]).swapaxes(0, 1)

    i = jnp.arange(RET_CHUNK, dtype=jnp.float32)
    rel = i[:, None] - i[None, :]
    causal = rel >= 0
    decay = jnp.where(causal[None], jnp.exp(jnp.where(causal, rel, 0.0)[None] * log_gamma[:, None, None]), 0.0)
    q_decay = jnp.exp((i[:, None] + 1.0) * log_gamma[None, :])
    k_decay = jnp.exp((RET_CHUNK - 1.0 - i)[:, None] * log_gamma[None, :])
    chunk_decay = jnp.exp(RET_CHUNK * log_gamma)

    def step(S, qkv):
        qc, kc, vc = qkv
        inner = jnp.einsum('bihd,bjhd->bhij', qc, kc) * decay[None]
        o = jnp.einsum('bhij,bjhe->bihe', inner, vc) + jnp.einsum('bihd,bhde->bihe', qc, S) * q_decay[None, :, :, None]
        S = chunk_decay[None, :, None, None] * S + jnp.einsum('bjhd,bjhe->bhde', kc * k_decay[None, :, :, None], vc)
        return S, o

    S, o = lax.scan(step, state0, (chunks(q), chunks(k), chunks(v)))
    return o.swapaxes(0, 1).reshape(B, L, *v.shape[2:]), S


def head_norm(o, g):
    mu = jnp.mean(o, axis=-1, keepdims=True)
    var = jnp.mean(jnp.square(o - mu), axis=-1, keepdims=True)
    y = (o - mu) * lax.rsqrt(var + EPS)
    return y.reshape(o.shape[0], o.shape[1], -1) * g.astype(jnp.float32)


def mixer(h, p, cache):
    B, L, _ = h.shape
    f32 = jnp.float32
    (cb, cc, cx, q, k, v, rq, rk, rv, rgf, rgb, ga, gb, gr) = split_in_proj(h @ p['w_in'])
    y_conv = cb * short_conv(cc * cx, p['conv_w'], p['conv_b'])
    q = q.reshape(B, L, N_Q_HEADS, HEAD_DIM)
    k = k.reshape(B, L, N_KV_HEADS, HEAD_DIM)
    v = v.reshape(B, L, N_KV_HEADS, HEAD_DIM)
    rq = rq.reshape(B, L, RET_HEADS, RET_DK).astype(f32)
    rk = rk.reshape(B, L, RET_HEADS, RET_DK).astype(f32) * RET_DK ** -0.5
    rv = rv.reshape(B, L, RET_HEADS, RET_DV).astype(f32)
    lg_f = jax.nn.log_sigmoid(p['ret_decay_f'].astype(f32))
    lg_b = jax.nn.log_sigmoid(p['ret_decay_b'].astype(f32))
    if cache is None:
        y_attn = context_attention(q, k, v, p['attn_sink'])
        s0_f = jnp.zeros((B, RET_HEADS, RET_DK, RET_DV), f32)
        s0_b = s0_f
    else:
        k_ctx, v_ctx, s0_f, s0_b = cache
        y_attn = banded_attention(axial_rope(q), axial_rope(k), v, k_ctx.astype(h.dtype), v_ctx.astype(h.dtype), p['attn_sink'])
        s0_f = s0_f.astype(f32)
        s0_b = s0_b.astype(f32)
    o_f, s_f = retention_scan(rq, rk, rv, lg_f, s0_f)
    o_b, s_b = retention_scan(jnp.flip(rq, 1), jnp.flip(rk, 1), jnp.flip(rv, 1), lg_b, s0_b)
    o_b = jnp.flip(o_b, 1)
    gn = p['ret_gn_g']
    y_ret = (jax.nn.silu(rgf) * head_norm(o_f, gn).astype(h.dtype)
             + jax.nn.silu(rgb) * head_norm(o_b, gn).astype(h.dtype))
    merged = (jax.nn.sigmoid(ga) * (y_conv @ p['w_conv_out'])
              + jax.nn.sigmoid(gb) * (y_attn @ p['w_attn_out'])
              + jax.nn.sigmoid(gr) * (y_ret @ p['w_ret_out']))
    out = merged @ p['w_o']
    ctx_state = (k, v, s_f.astype(h.dtype), s_b.astype(h.dtype)) if cache is None else None
    return out, ctx_state


def block(x, cvec, p, cache):
    sh1, sc1, g1, sh2, sc2, g2 = adaln(cvec, p)
    h = modulate(rms_norm(x, p['norm1_g']), sh1, sc1)
    m, ctx_state = mixer(h, p, cache)
    x = x + g1[:, None, :] * m
    h = modulate(rms_norm(x, p['norm2_g']), sh2, sc2)
    f = jnp.square(jax.nn.relu(h @ p['w_ff1'] + p['b_ff1'])) @ p['w_ff2'] + p['b_ff2']
    x = x + g2[:, None, :] * f
    return x, ctx_state


def setup_inputs(seed: int = 0) -> dict:
    key = jax.random.key(seed)
    ks = jax.random.split(key, 32)
    f32 = jnp.float32

    def nrm(k, shape, scale):
        return jax.random.normal(k, shape, f32) * scale

    gamma0 = 1.0 - 2.0 ** (-5.0 - np.arange(RET_HEADS, dtype=np.float32))
    logit0 = jnp.asarray(np.log(gamma0) - np.log1p(-gamma0), f32)
    return {
        'x_prompt': nrm(ks[0], (BATCH, SEQ, D_MODEL), 1.0),
        'x_sample': nrm(ks[1], (DEC_BATCH, DEC_SEQ, D_MODEL), 1.0),
        'c': nrm(ks[2], (DEC_BATCH, D_MODEL), 1.0),
        'cache_k': nrm(ks[3], (DEC_BATCH, DEPTH, PAST_LEN, N_KV_HEADS, HEAD_DIM), 1.0),
        'cache_v': nrm(ks[4], (DEC_BATCH, DEPTH, PAST_LEN, N_KV_HEADS, HEAD_DIM), 1.0),
        'state_ret_f': nrm(ks[5], (DEC_BATCH, DEPTH, RET_HEADS, RET_DK, RET_DV), 0.5),
        'state_ret_b': nrm(ks[6], (DEC_BATCH, DEPTH, RET_HEADS, RET_DK, RET_DV), 0.5),
        'c_ctx': nrm(ks[7], (D_MODEL,), 1.0),
        'w_mod': nrm(ks[8], (DEPTH, D_MODEL, N_MOD * D_MODEL), D_MODEL ** -0.5),
        'b_mod': nrm(ks[9], (DEPTH, N_MOD * D_MODEL), 0.01),
        'norm1_g': 1.0 + nrm(ks[10], (DEPTH, D_MODEL), 0.02),
        'w_in': nrm(ks[11], (DEPTH, D_MODEL, IN_COLS), D_MODEL ** -0.5),
        'conv_w': nrm(ks[12], (DEPTH, CONV_WIDTH, CONV_DIM), CONV_WIDTH ** -0.5),
        'conv_b': nrm(ks[13], (DEPTH, CONV_DIM), 0.01),
        'attn_sink': nrm(ks[14], (DEPTH, N_Q_HEADS), 0.5),
        'ret_decay_f': logit0[None, :] + nrm(ks[15], (DEPTH, RET_HEADS), 0.05),
        'ret_decay_b': logit0[None, :] + nrm(ks[16], (DEPTH, RET_HEADS), 0.05),
        'ret_gn_g': 1.0 + nrm(ks[17], (DEPTH, RET_V_DIM), 0.02),
        'w_conv_out': nrm(ks[18], (DEPTH, CONV_DIM, D_MODEL), CONV_DIM ** -0.5),
        'w_attn_out': nrm(ks[19], (DEPTH, ATTN_DIM, D_MODEL), ATTN_DIM ** -0.5),
        'w_ret_out': nrm(ks[20], (DEPTH, RET_V_DIM, D_MODEL), RET_V_DIM ** -0.5),
        'w_o': nrm(ks[21], (DEPTH, D_MODEL, D_MODEL), D_MODEL ** -0.5),
        'norm2_g': 1.0 + nrm(ks[22], (DEPTH, D_MODEL), 0.02),
        'w_ff1': nrm(ks[23], (DEPTH, D_MODEL, D_FF), D_MODEL ** -0.5),
        'b_ff1': nrm(ks[24], (DEPTH, D_FF), 0.01),
        'w_ff2': nrm(ks[25], (DEPTH, D_FF, D_MODEL), D_FF ** -0.5),
        'b_ff2': nrm(ks[26], (DEPTH, D_MODEL), 0.01),
        'final_g': 1.0 + nrm(ks[27], (D_MODEL,), 0.02),
    }


def reference(x_prompt, x_sample, c, cache_k, cache_v, state_ret_f, state_ret_b, c_ctx,
              w_mod, b_mod, norm1_g, w_in, conv_w, conv_b, attn_sink, ret_decay_f, ret_decay_b,
              ret_gn_g, w_conv_out, w_attn_out, w_ret_out, w_o, norm2_g, w_ff1, b_ff1, w_ff2, b_ff2,
              final_g):
    xp = x_prompt
    xs = x_sample
    ks_, vs_, sfs_, sbs_ = [], [], [], []
    for l in range(DEPTH):
        p = {'w_mod': w_mod[l], 'b_mod': b_mod[l], 'norm1_g': norm1_g[l], 'w_in': w_in[l],
             'conv_w': conv_w[l], 'conv_b': conv_b[l], 'attn_sink': attn_sink[l],
             'ret_decay_f': ret_decay_f[l], 'ret_decay_b': ret_decay_b[l], 'ret_gn_g': ret_gn_g[l],
             'w_conv_out': w_conv_out[l], 'w_attn_out': w_attn_out[l], 'w_ret_out': w_ret_out[l],
             'w_o': w_o[l], 'norm2_g': norm2_g[l], 'w_ff1': w_ff1[l], 'b_ff1': b_ff1[l],
             'w_ff2': w_ff2[l], 'b_ff2': b_ff2[l]}
        xp, (k_l, v_l, sf_l, sb_l) = block(xp, c_ctx[None, :], p, None)
        ks_.append(k_l)
        vs_.append(v_l)
        sfs_.append(sf_l)
        sbs_.append(sb_l)
        xs, _ = block(xs, c, p, (cache_k[:, l], cache_v[:, l], state_ret_f[:, l], state_ret_b[:, l]))
    y_prompt = rms_norm(xp, final_g)
    y_sample = rms_norm(xs, final_g)
    return (y_prompt, y_sample, jnp.stack(ks_, axis=1), jnp.stack(vs_, axis=1), jnp.stack(sfs_, axis=1), jnp.stack(sbs_, axis=1))
```

```python
import functools

import numpy as np
import jax
import jax.numpy as jnp
from jax import lax
from jax.experimental import pallas as pl
from jax.experimental.pallas import tpu as pltpu

F32 = jnp.float32
BF16 = jnp.bfloat16

D_MODEL = 4096
BATCH = 16
SEQ = 256
DEPTH = 2
DEC_BATCH = 8
DEC_SEQ = 1024
PAST_LEN = 256
GRID_W = 64
EPS = 1e-6
N_MOD = 6
CONV_DIM = 2048
N_Q_HEADS = 16
N_KV_HEADS = 4
GROUP = N_Q_HEADS // N_KV_HEADS
HEAD_DIM = 128
ATTN_DIM = N_Q_HEADS * HEAD_DIM
KV_DIM = N_KV_HEADS * HEAD_DIM
WINDOW = 128
BLOCK = 128
ROPE_BASE = 10000.0
RET_HEADS = 8
RET_DK = 256
RET_DV = 256
RET_V_DIM = RET_HEADS * RET_DV
RET_CHUNK = 128
D_FF = 4 * D_MODEL

T_CTX = BATCH * SEQ
T_LAT = DEC_BATCH * DEC_SEQ
T_ALL = T_CTX + T_LAT
N_COND = 1 + DEC_BATCH
COND_PAD = 16

OFF_CB = 0
OFF_CC = OFF_CB + CONV_DIM
OFF_CX = OFF_CC + CONV_DIM
OFF_Q = OFF_CX + CONV_DIM
OFF_K = OFF_Q + ATTN_DIM
OFF_V = OFF_K + KV_DIM
OFF_RQ = OFF_V + KV_DIM
OFF_RK = OFF_RQ + RET_HEADS * RET_DK
OFF_RV = OFF_RK + RET_HEADS * RET_DK
OFF_RGF = OFF_RV + RET_V_DIM
OFF_RGB = OFF_RGF + RET_V_DIM
OFF_GA = OFF_RGB + RET_V_DIM
OFF_GB = OFF_GA + D_MODEL
OFF_GR = OFF_GB + D_MODEL
IN_COLS = OFF_GR + D_MODEL

V7X_VMEM_LIMIT_CAP = 60000 * 1024
BM = 1024
BN = 1024
BK_FF2 = 2048
BM_NORM = 256
BN_MERGE = 512
BN_ADALN = 512
BN_CONV = 512
KV_BLOCK = OFF_K // BN
assert OFF_K % BN == 0 and OFF_V + KV_DIM == OFF_K + BN
assert BM == DEC_SEQ and T_CTX % BM == 0 and BM % SEQ == 0


def _vmem_limit(block_bytes, scratch_bytes=0, temp_bytes=0):
    need = 2 * sum(block_bytes) + scratch_bytes + temp_bytes + (2 << 20)
    return int(min(need, V7X_VMEM_LIMIT_CAP))


def _params(semantics, vmem):
    return pltpu.CompilerParams(dimension_semantics=semantics, vmem_limit_bytes=vmem)


def _cond_row(i, bm):
    n_ctx = T_CTX // bm
    per_seq = DEC_SEQ // bm
    return jnp.where(i < n_ctx, 0, 1 + (i - n_ctx) // per_seq)


def _mod_spec(layer, which, bm, bn=D_MODEL, with_j=False):
    per = D_MODEL // bn
    if with_j:
        return pl.BlockSpec((None, None, 1, bn),
                            lambda i, j, *_: (layer, _cond_row(i, bm), 0, which * per + j))
    return pl.BlockSpec((None, None, 1, bn), lambda i, *_: (layer, _cond_row(i, bm), 0, which * per))


def _adaln_kernel(c_ref, w_ref, b_ref, o_ref):
    s = c_ref[...]
    s = s * jax.nn.sigmoid(s)
    acc = jnp.dot(s.astype(BF16), w_ref[...].astype(BF16), preferred_element_type=F32)
    o_ref[...] = acc + b_ref[...]


def _adaln(cond, w_mod, b_mod):
    n = N_MOD * D_MODEL
    return pl.pallas_call(
        _adaln_kernel,
        grid=(DEPTH, n // BN_ADALN),
        in_specs=[pl.BlockSpec((COND_PAD, D_MODEL), lambda l, j: (0, 0)),
                  pl.BlockSpec((None, D_MODEL, BN_ADALN), lambda l, j: (l, 0, j)),
                  pl.BlockSpec((None, 1, BN_ADALN), lambda l, j: (l, 0, j))],
        out_specs=pl.BlockSpec((None, COND_PAD, BN_ADALN), lambda l, j: (l, 0, j)),
        out_shape=jax.ShapeDtypeStruct((DEPTH, COND_PAD, n), F32),
        compiler_params=_params(("arbitrary", "arbitrary"),
                                _vmem_limit([D_MODEL * BN_ADALN * 4], temp_bytes=D_MODEL * BN_ADALN * 2)),
        name="adaln",
    )(cond, w_mod, b_mod.reshape(DEPTH, 1, n))


def _rms(x, g):
    y = x * lax.rsqrt(jnp.mean(x * x, axis=-1, keepdims=True) + EPS)
    return y * g


def _norm_mod_kernel(x_ref, g_ref, sh_ref, sc_ref, o_ref):
    y = _rms(x_ref[...], g_ref[...])
    o_ref[...] = (y * (1 + sc_ref[...]) + sh_ref[...]).astype(o_ref.dtype)


def _norm_mod(x, g, mod4, layer, which_shift):
    return pl.pallas_call(
        _norm_mod_kernel,
        grid=(T_ALL // BM_NORM,),
        in_specs=[pl.BlockSpec((BM_NORM, D_MODEL), lambda i: (i, 0)),
                  pl.BlockSpec((None, 1, D_MODEL), lambda i: (layer, 0, 0)),
                  _mod_spec(layer, which_shift, BM_NORM),
                  _mod_spec(layer, which_shift + 1, BM_NORM)],
        out_specs=pl.BlockSpec((BM_NORM, D_MODEL), lambda i: (i, 0)),
        out_shape=jax.ShapeDtypeStruct((T_ALL, D_MODEL), BF16),
        compiler_params=_params(("arbitrary",), _vmem_limit([BM_NORM * D_MODEL * 4, BM_NORM * D_MODEL * 2],
                                                            temp_bytes=2 * BM_NORM * D_MODEL * 4)),
        name="norm_mod",
    )(x, g.reshape(DEPTH, 1, D_MODEL), mod4, mod4)


def _norm_kernel(x_ref, g_ref, o_ref):
    o_ref[...] = _rms(x_ref[...], g_ref[...])


def _final_norm(x, g, row0, rows):
    blk0 = row0 // BM_NORM
    return pl.pallas_call(
        _norm_kernel,
        grid=(rows // BM_NORM,),
        in_specs=[pl.BlockSpec((BM_NORM, D_MODEL), lambda i: (blk0 + i, 0)),
                  pl.BlockSpec((1, D_MODEL), lambda i: (0, 0))],
        out_specs=pl.BlockSpec((BM_NORM, D_MODEL), lambda i: (i, 0)),
        out_shape=jax.ShapeDtypeStruct((rows, D_MODEL), F32),
        compiler_params=_params(("arbitrary",), _vmem_limit([BM_NORM * D_MODEL * 4] * 2,
                                                            temp_bytes=2 * BM_NORM * D_MODEL * 4)),
        name="final_norm",
    )(x, g.reshape(1, D_MODEL))


def _in_proj_kernel(h_ref, w_ref, z_ref, kv_ref):
    acc = jnp.dot(h_ref[...], w_ref[...], preferred_element_type=F32)
    z_ref[...] = acc.astype(z_ref.dtype)

    @pl.when(pl.program_id(1) == KV_BLOCK)
    def _():
        kv_ref[...] = acc


def _in_proj(h, w_in, layer):
    return pl.pallas_call(
        _in_proj_kernel,
        grid=(T_ALL // BM, IN_COLS // BN),
        in_specs=[pl.BlockSpec((BM, D_MODEL), lambda i, j: (i, 0)),
                  pl.BlockSpec((None, D_MODEL, BN), lambda i, j: (layer, 0, j))],
        out_specs=[pl.BlockSpec((BM, BN), lambda i, j: (i, j)),
                   pl.BlockSpec((BM, BN), lambda i, j: (i, 0))],
        out_shape=[jax.ShapeDtypeStruct((T_ALL, IN_COLS), BF16),
                   jax.ShapeDtypeStruct((T_ALL, BN), F32)],
        compiler_params=_params(("arbitrary", "arbitrary"),
                                _vmem_limit([BM * D_MODEL * 2, D_MODEL * BN * 2, BM * BN * 2, BM * BN * 4],
                                            temp_bytes=2 * BM * BN * 4)),
        name="in_proj",
    )(h, w_in)


def _conv_kernel(cb_ref, cc_ref, cx_ref, w_ref, b_ref, o_ref):
    i = pl.program_id(0)
    seq = jnp.where(i < T_CTX // BM, SEQ, DEC_SEQ)
    u = cc_ref[...].astype(F32) * cx_ref[...].astype(F32)
    pos = lax.broadcasted_iota(jnp.int32, u.shape, 0) & (seq - 1)
    prev = jnp.where(pos == 0, 0.0, pltpu.roll(u, 1, 0))
    nxt = jnp.where(pos == seq - 1, 0.0, pltpu.roll(u, BM - 1, 0))
    w = w_ref[...]
    conv = w[0:1] * prev + w[1:2] * u + w[2:3] * nxt + b_ref[...]
    o_ref[...] = (cb_ref[...].astype(F32) * conv).astype(o_ref.dtype)


def _conv_branch(z, conv_w, conv_b, layer):
    nb = CONV_DIM // BN_CONV
    zspec = lambda off: pl.BlockSpec((BM, BN_CONV), lambda i, j: (i, off // BN_CONV + j))
    return pl.pallas_call(
        _conv_kernel,
        grid=(T_ALL // BM, nb),
        in_specs=[zspec(OFF_CB), zspec(OFF_CC), zspec(OFF_CX),
                  pl.BlockSpec((None, 3, BN_CONV), lambda i, j: (layer, 0, j)),
                  pl.BlockSpec((None, 1, BN_CONV), lambda i, j: (layer, 0, j))],
        out_specs=pl.BlockSpec((BM, BN_CONV), lambda i, j: (i, j)),
        out_shape=jax.ShapeDtypeStruct((T_ALL, CONV_DIM), BF16),
        compiler_params=_params(("arbitrary", "arbitrary"),
                                _vmem_limit([BM * BN_CONV * 2] * 4, temp_bytes=6 * BM * BN_CONV * 4)),
        name="conv_branch",
    )(z, z, z, conv_w, conv_b.reshape(DEPTH, 1, CONV_DIM))


def _sink_column(sink_ref, layer, kvh, rows_per_head):
    row = lax.broadcasted_iota(jnp.int32, (GROUP * rows_per_head, 1), 0)
    col = jnp.full(row.shape, sink_ref[layer, kvh * GROUP], F32)
    for g in range(1, GROUP):
        col = jnp.where(row >= g * rows_per_head, sink_ref[layer, kvh * GROUP + g], col)
    return col


def _stack_heads(q):
    return jnp.concatenate([q[:, g * HEAD_DIM:(g + 1) * HEAD_DIM] for g in range(GROUP)], axis=0)


def _qk(q, k):
    return lax.dot_general(q, k, (((1,), (1,)), ((), ())), preferred_element_type=F32)


def _ctx_attn_kernel(layer, sink_ref, q_ref, k_ref, v_ref, o_ref):
    kvh = pl.program_id(1)
    q4 = _stack_heads(q_ref[...])
    s = _qk(q4, k_ref[...]) * HEAD_DIM ** -0.5
    sink = _sink_column(sink_ref, layer, kvh, SEQ)
    m = jnp.maximum(jnp.max(s, axis=-1, keepdims=True), sink)
    e = jnp.exp(s - m)
    p = e / (jnp.sum(e, axis=-1, keepdims=True) + jnp.exp(sink - m))
    o = jnp.dot(p.astype(BF16), v_ref[...], preferred_element_type=F32)
    for g in range(GROUP):
        o_ref[:, g * HEAD_DIM:(g + 1) * HEAD_DIM] = o[g * SEQ:(g + 1) * SEQ].astype(o_ref.dtype)


def _ctx_attention(z, attn_sink, layer):
    qw = GROUP * HEAD_DIM
    return pl.pallas_call(
        functools.partial(_ctx_attn_kernel, layer),
        grid=(BATCH, N_KV_HEADS),
        in_specs=[pl.BlockSpec(memory_space=pltpu.SMEM),
                  pl.BlockSpec((SEQ, qw), lambda b, h: (b, OFF_Q // qw + h)),
                  pl.BlockSpec((SEQ, HEAD_DIM), lambda b, h: (b, OFF_K // HEAD_DIM + h)),
                  pl.BlockSpec((SEQ, HEAD_DIM), lambda b, h: (b, OFF_V // HEAD_DIM + h))],
        out_specs=pl.BlockSpec((SEQ, qw), lambda b, h: (b, h)),
        out_shape=jax.ShapeDtypeStruct((T_CTX, ATTN_DIM), BF16),
        compiler_params=_params(("arbitrary", "arbitrary"),
                                _vmem_limit([SEQ * qw * 2] * 2, temp_bytes=6 * GROUP * SEQ * SEQ * 4)),
        name="ctx_attention",
    )(attn_sink, z, z, z)


def _rope_tables():
    pos = np.arange(DEC_SEQ)
    row_id = jnp.asarray(pos // GRID_W, F32)
    col_id = jnp.asarray(pos % GRID_W, F32)
    half = HEAD_DIM // 2
    freqs = ROPE_BASE ** (-jnp.arange(0, half, 2, dtype=F32) / half)
    ang_r = row_id[:, None] * freqs[None, :]
    ang_c = col_id[:, None] * freqs[None, :]
    cos = jnp.concatenate([jnp.cos(ang_r), jnp.cos(ang_r), jnp.cos(ang_c), jnp.cos(ang_c)], axis=-1)
    sin = jnp.concatenate([-jnp.sin(ang_r), jnp.sin(ang_r), -jnp.sin(ang_c), jnp.sin(ang_c)], axis=-1)
    return cos, sin


def _rope_kernel(x_ref, cos_ref, sin_ref, o_ref):
    cos = cos_ref[...]
    sin = sin_ref[...]
    quarter = HEAD_DIM // 4
    lane = lax.broadcasted_iota(jnp.int32, cos.shape, 1)
    first = (lane % (2 * quarter)) < quarter
    for g in range(GROUP):
        x = x_ref[:, g * HEAD_DIM:(g + 1) * HEAD_DIM].astype(F32)
        partner = jnp.where(first, pltpu.roll(x, HEAD_DIM - quarter, 1), pltpu.roll(x, quarter, 1))
        o_ref[:, g * HEAD_DIM:(g + 1) * HEAD_DIM] = (x * cos + partner * sin).astype(o_ref.dtype)


def _rope_qk(z, cos, sin):
    w = GROUP * HEAD_DIM
    nb = (ATTN_DIM + KV_DIM) // w
    tab = pl.BlockSpec((DEC_SEQ, HEAD_DIM), lambda b, j: (0, 0))
    return pl.pallas_call(
        _rope_kernel,
        grid=(DEC_BATCH, nb),
        in_specs=[pl.BlockSpec((DEC_SEQ, w), lambda b, j: (T_CTX // DEC_SEQ + b, OFF_Q // w + j)), tab, tab],
        out_specs=pl.BlockSpec((DEC_SEQ, w), lambda b, j: (b, j)),
        out_shape=jax.ShapeDtypeStruct((T_LAT, ATTN_DIM + KV_DIM), BF16),
        compiler_params=_params(("arbitrary", "arbitrary"),
                                _vmem_limit([DEC_SEQ * w * 2] * 2 + [DEC_SEQ * HEAD_DIM * 4] * 2,
                                            temp_bytes=6 * DEC_SEQ * HEAD_DIM * 4)),
        name="rope_qk",
    )(z, cos, sin)


def _lat_attn_kernel(layer, sink_ref, q_ref, k_ref, v_ref, kc_ref, vc_ref, o_ref):
    kvh = pl.program_id(1)
    nblk = DEC_SEQ // BLOCK
    scale = HEAD_DIM ** -0.5
    sink = _sink_column(sink_ref, layer, kvh, BLOCK)
    k_ctx = kc_ref[...].astype(BF16)
    v_ctx = vc_ref[...].astype(BF16)
    for n in range(nblk):
        lo = max(n - 1, 0) * BLOCK
        hi = min(n + 2, nblk) * BLOCK
        q4 = _stack_heads(q_ref[n * BLOCK:(n + 1) * BLOCK, :])
        s_loc = _qk(q4, k_ref[lo:hi, :]) * scale
        qpos = n * BLOCK + lax.broadcasted_iota(jnp.int32, s_loc.shape, 0) % BLOCK
        kpos = lo + lax.broadcasted_iota(jnp.int32, s_loc.shape, 1)
        s_loc = jnp.where(jnp.abs(kpos - qpos) <= WINDOW, s_loc, -jnp.inf)
        s_ctx = _qk(q4, k_ctx) * scale
        m = jnp.maximum(jnp.maximum(jnp.max(s_loc, axis=-1, keepdims=True),
                                    jnp.max(s_ctx, axis=-1, keepdims=True)), sink)
        e_loc = jnp.exp(s_loc - m)
        e_ctx = jnp.exp(s_ctx - m)
        den = (jnp.sum(e_loc, axis=-1, keepdims=True) + jnp.sum(e_ctx, axis=-1, keepdims=True)
               + jnp.exp(sink - m))
        o = (jnp.dot((e_loc / den).astype(BF16), v_ref[lo:hi, :], preferred_element_type=F32)
             + jnp.dot((e_ctx / den).astype(BF16), v_ctx, preferred_element_type=F32))
        for g in range(GROUP):
            o_ref[n * BLOCK:(n + 1) * BLOCK, g * HEAD_DIM:(g + 1) * HEAD_DIM] = (
                o[g * BLOCK:(g + 1) * BLOCK].astype(o_ref.dtype))


def _lat_attention(qk_rot, z, cache_k, cache_v, attn_sink, layer):
    qw = GROUP * HEAD_DIM
    ck = cache_k.reshape(DEC_BATCH, DEPTH, PAST_LEN, KV_DIM)
    cv = cache_v.reshape(DEC_BATCH, DEPTH, PAST_LEN, KV_DIM)
    cspec = pl.BlockSpec((None, None, PAST_LEN, HEAD_DIM), lambda b, h: (b, layer, 0, h))
    return pl.pallas_call(
        functools.partial(_lat_attn_kernel, layer),
        grid=(DEC_BATCH, N_KV_HEADS),
        in_specs=[pl.BlockSpec(memory_space=pltpu.SMEM),
                  pl.BlockSpec((DEC_SEQ, qw), lambda b, h: (b, h)),
                  pl.BlockSpec((DEC_SEQ, HEAD_DIM), lambda b, h: (b, ATTN_DIM // HEAD_DIM + h)),
                  pl.BlockSpec((DEC_SEQ, HEAD_DIM), lambda b, h: (T_CTX // DEC_SEQ + b, OFF_V // HEAD_DIM + h)),
                  cspec, cspec],
        out_specs=pl.BlockSpec((DEC_SEQ, qw), lambda b, h: (b, h)),
        out_shape=jax.ShapeDtypeStruct((T_LAT, ATTN_DIM), BF16),
        compiler_params=_params(("arbitrary", "arbitrary"),
                                _vmem_limit([DEC_SEQ * qw * 2] * 2 + [PAST_LEN * HEAD_DIM * 4] * 2,
                                            temp_bytes=16 << 20)),
        name="lat_attention",
    )(attn_sink, qk_rot, qk_rot, z, ck, cv)


def _log_sigmoid(x):
    return jnp.minimum(x, 0.0) - jnp.log(1.0 + jnp.exp(-jnp.abs(x)))


def _head_norm(o, g):
    mu = jnp.mean(o, axis=-1, keepdims=True)
    d = o - mu
    var = jnp.mean(d * d, axis=-1, keepdims=True)
    return d * lax.rsqrt(var + EPS) * g


def _silu(x):
    return x * jax.nn.sigmoid(x)


def _kt_v(k, v):
    return lax.dot_general(k, v, (((0,), (0,)), ((), ())), preferred_element_type=F32)


def _retention_kernel(seq_len, has_state, *refs):
    if has_state:
        (df_ref, db_ref, rq_ref, rk_ref, rv_ref, gf_ref, gb_ref, gn_ref, s0f_ref, s0b_ref,
         y_ref, of_scr, s_scr) = refs
    else:
        (df_ref, db_ref, rq_ref, rk_ref, rv_ref, gf_ref, gb_ref, gn_ref,
         y_ref, sf_ref, sb_ref, of_scr, s_scr) = refs
    n = seq_len // RET_CHUNK
    c = RET_CHUNK
    lg_f = _log_sigmoid(df_ref[...])
    lg_b = _log_sigmoid(db_ref[...])
    ii = lax.broadcasted_iota(jnp.int32, (c, c), 0)
    jj = lax.broadcasted_iota(jnp.int32, (c, c), 1)
    row = lax.broadcasted_iota(jnp.int32, (c, RET_DV), 0).astype(F32)

    def decay_matrix(rel, lg):
        keep = rel >= 0
        return jnp.where(keep, jnp.exp(jnp.where(keep, rel, 0).astype(F32) * lg[:, :c]), 0.0)

    def chunk(r0, lg, dmat, q_dec, k_dec):
        qc = rq_ref[pl.ds(r0, c), :]
        ks = rk_ref[pl.ds(r0, c), :].astype(F32) * RET_DK ** -0.5
        vc = rv_ref[pl.ds(r0, c), :]
        s_old = s_scr[...]
        inner = _qk(qc, ks.astype(BF16)) * dmat
        o = (jnp.dot(inner.astype(BF16), vc, preferred_element_type=F32)
             + jnp.dot(qc, s_old.astype(BF16), preferred_element_type=F32) * q_dec)
        s_scr[...] = jnp.exp(c * lg) * s_old + _kt_v((ks * k_dec).astype(BF16), vc)
        return o

    if has_state:
        s_scr[...] = s0f_ref[...]
    else:
        s_scr[...] = jnp.zeros_like(s_scr)
    dmat_f = decay_matrix(ii - jj, lg_f)
    qdec_f = jnp.exp((row + 1.0) * lg_f)
    kdec_f = jnp.exp((c - 1.0 - row) * lg_f)

    def fwd(t, carry):
        r0 = pl.multiple_of(t * c, c)
        of_scr[pl.ds(r0, c), :] = chunk(r0, lg_f, dmat_f, qdec_f, kdec_f)
        return carry

    lax.fori_loop(0, n, fwd, 0)
    if not has_state:
        sf_ref[...] = s_scr[...]

    if has_state:
        s_scr[...] = s0b_ref[...]
    else:
        s_scr[...] = jnp.zeros_like(s_scr)
    dmat_b = decay_matrix(jj - ii, lg_b)
    qdec_b = jnp.exp((c - row) * lg_b)
    kdec_b = jnp.exp(row * lg_b)
    gn = gn_ref[...]

    def bwd(t, carry):
        r0 = pl.multiple_of((n - 1 - t) * c, c)
        o_b = chunk(r0, lg_b, dmat_b, qdec_b, kdec_b)
        o_f = of_scr[pl.ds(r0, c), :]
        y = (_silu(gf_ref[pl.ds(r0, c), :].astype(F32)) * _head_norm(o_f, gn)
             + _silu(gb_ref[pl.ds(r0, c), :].astype(F32)) * _head_norm(o_b, gn))
        y_ref[pl.ds(r0, c), :] = y.astype(y_ref.dtype)
        return carry

    lax.fori_loop(0, n, bwd, 0)
    if not has_state:
        sb_ref[...] = s_scr[...]


def _retention(z, dec_f, dec_b, gn_g, layer, seq_len, row0, n_seq, states=None):
    has_state = states is not None
    blk0 = row0 // seq_len
    w = RET_DV
    zspec = lambda off: pl.BlockSpec((seq_len, w), lambda b, h: (blk0 + b, off // w + h))
    pspec = pl.BlockSpec((None, None, 1, w), lambda b, h: (layer, h, 0, 0))
    in_specs = [pspec, pspec, zspec(OFF_RQ), zspec(OFF_RK), zspec(OFF_RV), zspec(OFF_RGF), zspec(OFF_RGB),
                pl.BlockSpec((None, None, 1, w), lambda b, h: (layer, h, 0, 0))]
    args = [dec_f, dec_b, z, z, z, z, z, gn_g.reshape(DEPTH, RET_HEADS, 1, w)]
    y_spec = pl.BlockSpec((seq_len, w), lambda b, h: (b, h))
    y_shape = jax.ShapeDtypeStruct((n_seq * seq_len, RET_V_DIM), BF16)
    st_bytes = RET_DK * RET_DV * 4
    if has_state:
        sspec = pl.BlockSpec((None, None, None, RET_DK, RET_DV), lambda b, h: (b, layer, h, 0, 0))
        in_specs += [sspec, sspec]
        args += list(states)
        out_specs, out_shape = y_spec, y_shape
    else:
        ospec = pl.BlockSpec((None, None, RET_DK, RET_DV), lambda b, h: (b, h, 0, 0))
        oshape = jax.ShapeDtypeStruct((n_seq, RET_HEADS, RET_DK, RET_DV), F32)
        out_specs, out_shape = [y_spec, ospec, ospec], [y_shape, oshape, oshape]
    return pl.pallas_call(
        functools.partial(_retention_kernel, seq_len, has_state),
        grid=(n_seq, RET_HEADS),
        in_specs=in_specs,
        out_specs=out_specs,
        out_shape=out_shape,
        scratch_shapes=[pltpu.VMEM((seq_len, w), F32), pltpu.VMEM((RET_DK, RET_DV), F32)],
        compiler_params=_params(("arbitrary", "arbitrary"),
                                _vmem_limit([seq_len * w * 2] * 6 + [st_bytes] * 2,
                                            scratch_bytes=seq_len * w * 4 + st_bytes, temp_bytes=8 << 20)),
        name="retention",
    )(*args)


def _merge_kernel(yc_ref, ya_ref, yr_ref, wc_ref, wa_ref, wr_ref, ga_ref, gb_ref, gr_ref, o_ref):
    def term(y_ref, w_ref, gate_ref):
        proj = jnp.dot(y_ref[...], w_ref[...], preferred_element_type=F32)
        return jax.nn.sigmoid(gate_ref[...].astype(F32)) * proj

    merged = term(yc_ref, wc_ref, ga_ref) + term(ya_ref, wa_ref, gb_ref) + term(yr_ref, wr_ref, gr_ref)
    o_ref[...] = merged.astype(o_ref.dtype)


def _merge(y_conv, y_attn, y_ret, w_conv_out, w_attn_out, w_ret_out, z, layer):
    bn = BN_MERGE
    yspec = pl.BlockSpec((BM, CONV_DIM), lambda i, j: (i, 0))
    wspec = pl.BlockSpec((None, CONV_DIM, bn), lambda i, j: (layer, 0, j))
    gspec = lambda off: pl.BlockSpec((BM, bn), lambda i, j: (i, off // bn + j))
    return pl.pallas_call(
        _merge_kernel,
        grid=(T_ALL // BM, D_MODEL // bn),
        in_specs=[yspec, yspec, yspec, wspec, wspec, wspec, gspec(OFF_GA), gspec(OFF_GB), gspec(OFF_GR)],
        out_specs=pl.BlockSpec((BM, bn), lambda i, j: (i, j)),
        out_shape=jax.ShapeDtypeStruct((T_ALL, D_MODEL), BF16),
        compiler_params=_params(("arbitrary", "arbitrary"),
                                _vmem_limit([BM * CONV_DIM * 2] * 3 + [CONV_DIM * bn * 2] * 3 + [BM * bn * 2] * 4,
                                            temp_bytes=3 * BM * bn * 4)),
        name="merge",
    )(y_conv, y_attn, y_ret, w_conv_out, w_attn_out, w_ret_out, z, z, z)


def _wo_kernel(m_ref, w_ref, x_ref, g_ref, o_ref):
    acc = jnp.dot(m_ref[...], w_ref[...], preferred_element_type=F32)
    o_ref[...] = x_ref[...] + g_ref[...] * acc


def _out_proj(merged, w_o, x, mod4, layer):
    return pl.pallas_call(
        _wo_kernel,
        grid=(T_ALL // BM, D_MODEL // BN),
        in_specs=[pl.BlockSpec((BM, D_MODEL), lambda i, j: (i, 0)),
                  pl.BlockSpec((None, D_MODEL, BN), lambda i, j: (layer, 0, j)),
                  pl.BlockSpec((BM, BN), lambda i, j: (i, j)),
                  _mod_spec(layer, 2, BM, BN, with_j=True)],
        out_specs=pl.BlockSpec((BM, BN), lambda i, j: (i, j)),
        out_shape=jax.ShapeDtypeStruct((T_ALL, D_MODEL), F32),
        compiler_params=_params(("arbitrary", "arbitrary"),
                                _vmem_limit([BM * D_MODEL * 2, D_MODEL * BN * 2, BM * BN * 4, BM * BN * 4],
                                            temp_bytes=2 * BM * BN * 4)),
        name="out_proj",
    )(merged, w_o, x, mod4)


def _ff1_kernel(h_ref, w_ref, b_ref, o_ref):
    acc = jnp.dot(h_ref[...], w_ref[...], preferred_element_type=F32) + b_ref[...]
    r = jnp.maximum(acc, 0.0)
    o_ref[...] = (r * r).astype(o_ref.dtype)


def _ff1(h, w_ff1, b_ff1, layer):
    return pl.pallas_call(
        _ff1_kernel,
        grid=(T_ALL // BM, D_FF // BN),
        in_specs=[pl.BlockSpec((BM, D_MODEL), lambda i, j: (i, 0)),
                  pl.BlockSpec((None, D_MODEL, BN), lambda i, j: (layer, 0, j)),
                  pl.BlockSpec((None, 1, BN), lambda i, j: (layer, 0, j))],
        out_specs=pl.BlockSpec((BM, BN), lambda i, j: (i, j)),
        out_shape=jax.ShapeDtypeStruct((T_ALL, D_FF), BF16),
        compiler_params=_params(("arbitrary", "arbitrary"),
                                _vmem_limit([BM * D_MODEL * 2, D_MODEL * BN * 2, BM * BN * 2],
                                            temp_bytes=2 * BM * BN * 4)),
        name="ff1",
    )(h, w_ff1, b_ff1.reshape(DEPTH, 1, D_FF))


def _ff2_kernel(f_ref, w_ref, b_ref, x_ref, g_ref, o_ref, acc_ref):
    k = pl.program_id(2)
    part = jnp.dot(f_ref[...], w_ref[...], preferred_element_type=F32)

    @pl.when(k == 0)
    def _():
        acc_ref[...] = part

    @pl.when(k > 0)
    def _():
        acc_ref[...] += part

    @pl.when(k == pl.num_programs(2) - 1)
    def _():
        o_ref[...] = x_ref[...] + g_ref[...] * (acc_ref[...] + b_ref[...])


def _ff2(f, w_ff2, b_ff2, x, mod4, layer):
    bk = BK_FF2
    per = D_MODEL // BN
    return pl.pallas_call(
        _ff2_kernel,
        grid=(T_ALL // BM, D_MODEL // BN, D_FF // bk),
        in_specs=[pl.BlockSpec((BM, bk), lambda i, j, k: (i, k)),
                  pl.BlockSpec((None, bk, BN), lambda i, j, k: (layer, k, j)),
                  pl.BlockSpec((None, 1, BN), lambda i, j, k: (layer, 0, j)),
                  pl.BlockSpec((BM, BN), lambda i, j, k: (i, j)),
                  pl.BlockSpec((None, None, 1, BN), lambda i, j, k: (layer, _cond_row(i, BM), 0, 5 * per + j))],
        out_specs=pl.BlockSpec((BM, BN), lambda i, j, k: (i, j)),
        out_shape=jax.ShapeDtypeStruct((T_ALL, D_MODEL), F32),
        scratch_shapes=[pltpu.VMEM((BM, BN), F32)],
        compiler_params=_params(("arbitrary", "arbitrary", "arbitrary"),
                                _vmem_limit([BM * bk * 2, bk * BN * 2, BM * BN * 4, BM * BN * 4],
                                            scratch_bytes=BM * BN * 4, temp_bytes=2 * BM * BN * 4)),
        name="ff2",
    )(f, w_ff2, b_ff2.reshape(DEPTH, 1, D_MODEL), x, mod4)


def _ret_param(p):
    return jnp.broadcast_to(p.astype(F32)[:, :, None, None], (DEPTH, RET_HEADS, 1, RET_DV))


def kernel(x_prompt, x_sample, c, cache_k, cache_v, state_ret_f, state_ret_b, c_ctx, w_mod, b_mod, norm1_g, w_in, conv_w, conv_b, attn_sink, ret_decay_f, ret_decay_b, ret_gn_g, w_conv_out, w_attn_out, w_ret_out, w_o, norm2_g, w_ff1, b_ff1, w_ff2, b_ff2, final_g):
    x = jnp.concatenate([x_prompt.reshape(T_CTX, D_MODEL), x_sample.reshape(T_LAT, D_MODEL)], axis=0)
    cond = jnp.concatenate([c_ctx[None, :], c, jnp.zeros((COND_PAD - N_COND, D_MODEL), F32)], axis=0)
    mod4 = _adaln(cond, w_mod, b_mod).reshape(DEPTH, COND_PAD, 1, N_MOD * D_MODEL)

    w_in_b = w_in.astype(BF16)
    w_conv_b = w_conv_out.astype(BF16)
    w_attn_b = w_attn_out.astype(BF16)
    w_ret_b = w_ret_out.astype(BF16)
    w_o_b = w_o.astype(BF16)
    w_ff1_b = w_ff1.astype(BF16)
    w_ff2_b = w_ff2.astype(BF16)
    dec_f = _ret_param(ret_decay_f)
    dec_b = _ret_param(ret_decay_b)
    cos, sin = _rope_tables()

    ks, vs, sfs, sbs = [], [], [], []
    for l in range(DEPTH):
        h = _norm_mod(x, norm1_g, mod4, l, 0)
        z, kv = _in_proj(h, w_in_b, l)
        ks.append(kv[:T_CTX, :KV_DIM].reshape(BATCH, SEQ, N_KV_HEADS, HEAD_DIM))
        vs.append(kv[:T_CTX, KV_DIM:].reshape(BATCH, SEQ, N_KV_HEADS, HEAD_DIM))

        y_conv = _conv_branch(z, conv_w, conv_b, l)
        y_attn = jnp.concatenate([
            _ctx_attention(z, attn_sink, l),
            _lat_attention(_rope_qk(z, cos, sin), z, cache_k, cache_v, attn_sink, l)], axis=0)
        y_ret_ctx, s_f, s_b = _retention(z, dec_f, dec_b, ret_gn_g, l, SEQ, 0, BATCH)
        y_ret_lat = _retention(z, dec_f, dec_b, ret_gn_g, l, DEC_SEQ, T_CTX, DEC_BATCH,
                               states=(state_ret_f, state_ret_b))
        y_ret = jnp.concatenate([y_ret_ctx, y_ret_lat], axis=0)
        sfs.append(s_f)
        sbs.append(s_b)

        merged = _merge(y_conv, y_attn, y_ret, w_conv_b, w_attn_b, w_ret_b, z, l)
        x = _out_proj(merged, w_o_b, x, mod4, l)
        h = _norm_mod(x, norm2_g, mod4, l, 3)
        f = _ff1(h, w_ff1_b, b_ff1, l)
        x = _ff2(f, w_ff2_b, b_ff2, x, mod4, l)

    y_prompt = _final_norm(x, final_g, 0, T_CTX).reshape(BATCH, SEQ, D_MODEL)
    y_sample = _final_norm(x, final_g, T_CTX, T_LAT).reshape(DEC_BATCH, DEC_SEQ, D_MODEL)
    return (y_prompt, y_sample, jnp.stack(ks, axis=1), jnp.stack(vs, axis=1),
            jnp.stack(sfs, axis=1), jnp.stack(sbs, axis=1))
```

```python
import functools

import numpy as np
import jax
import jax.numpy as jnp
from jax import lax
from jax.experimental import pallas as pl
from jax.experimental.pallas import tpu as pltpu

F32 = jnp.float32
BF16 = jnp.bfloat16

D_MODEL = 4096
BATCH = 16
SEQ = 256
DEPTH = 2
DEC_BATCH = 8
DEC_SEQ = 1024
PAST_LEN = 256
GRID_W = 64
EPS = 1e-6
N_MOD = 6
CONV_DIM = 2048
N_Q_HEADS = 16
N_KV_HEADS = 4
GROUP = N_Q_HEADS // N_KV_HEADS
HEAD_DIM = 128
ATTN_DIM = N_Q_HEADS * HEAD_DIM
KV_DIM = N_KV_HEADS * HEAD_DIM
WINDOW = 128
BLOCK = 128
ROPE_BASE = 10000.0
RET_HEADS = 8
RET_DK = 256
RET_DV = 256
RET_V_DIM = RET_HEADS * RET_DV
RET_CHUNK = 128
D_FF = 4 * D_MODEL

T_CTX = BATCH * SEQ
T_LAT = DEC_BATCH * DEC_SEQ
T_ALL = T_CTX + T_LAT
N_COND = 1 + DEC_BATCH
COND_PAD = 16

OFF_CB = 0
OFF_CC = OFF_CB + CONV_DIM
OFF_CX = OFF_CC + CONV_DIM
OFF_Q = OFF_CX + CONV_DIM
OFF_K = OFF_Q + ATTN_DIM
OFF_V = OFF_K + KV_DIM
OFF_RQ = OFF_V + KV_DIM
OFF_RK = OFF_RQ + RET_HEADS * RET_DK
OFF_RV = OFF_RK + RET_HEADS * RET_DK
OFF_RGF = OFF_RV + RET_V_DIM
OFF_RGB = OFF_RGF + RET_V_DIM
OFF_GA = OFF_RGB + RET_V_DIM
OFF_GB = OFF_GA + D_MODEL
OFF_GR = OFF_GB + D_MODEL
IN_COLS = OFF_GR + D_MODEL

V7X_VMEM_LIMIT_CAP = 60000 * 1024
BM = 1024
BN = 1024
BK_FF2 = 2048
BM_NORM = 256
BN_MERGE = 512
BN_ADALN = 512
BN_CONV = 512
RET_HEADS_PER_STEP_CTX = 4
KV_BLOCK = OFF_K // BN
assert OFF_K % BN == 0 and OFF_V + KV_DIM == OFF_K + BN
assert BM == DEC_SEQ and T_CTX % BM == 0 and BM % SEQ == 0


def _vmem_limit(block_bytes, scratch_bytes=0, temp_bytes=0):
    need = 2 * sum(block_bytes) + scratch_bytes + temp_bytes + (2 << 20)
    return int(min(need, V7X_VMEM_LIMIT_CAP))


def _params(semantics, vmem):
    return pltpu.CompilerParams(dimension_semantics=semantics, vmem_limit_bytes=vmem)


def _cond_row(i, bm):
    n_ctx = T_CTX // bm
    per_seq = DEC_SEQ // bm
    return jnp.where(i < n_ctx, 0, 1 + (i - n_ctx) // per_seq)


def _mod_spec(layer, which, bm, bn=D_MODEL, with_j=False):
    per = D_MODEL // bn
    if with_j:
        return pl.BlockSpec((None, None, 1, bn),
                            lambda i, j, *_: (layer, _cond_row(i, bm), 0, which * per + j))
    return pl.BlockSpec((None, None, 1, bn), lambda i, *_: (layer, _cond_row(i, bm), 0, which * per))


def _adaln_kernel(c_ref, w_ref, b_ref, o_ref):
    s = c_ref[...]
    s = s * jax.nn.sigmoid(s)
    acc = jnp.dot(s.astype(BF16), w_ref[...].astype(BF16), preferred_element_type=F32)
    o_ref[...] = acc + b_ref[...]


def _adaln(cond, w_mod, b_mod):
    n = N_MOD * D_MODEL
    return pl.pallas_call(
        _adaln_kernel,
        grid=(DEPTH, n // BN_ADALN),
        in_specs=[pl.BlockSpec((COND_PAD, D_MODEL), lambda l, j: (0, 0)),
                  pl.BlockSpec((None, D_MODEL, BN_ADALN), lambda l, j: (l, 0, j)),
                  pl.BlockSpec((None, 1, BN_ADALN), lambda l, j: (l, 0, j))],
        out_specs=pl.BlockSpec((None, COND_PAD, BN_ADALN), lambda l, j: (l, 0, j)),
        out_shape=jax.ShapeDtypeStruct((DEPTH, COND_PAD, n), F32),
        compiler_params=_params(("arbitrary", "arbitrary"),
                                _vmem_limit([D_MODEL * BN_ADALN * 4], temp_bytes=D_MODEL * BN_ADALN * 2)),
        name="adaln",
    )(cond, w_mod, b_mod.reshape(DEPTH, 1, n))


def _rms(x, g):
    y = x * lax.rsqrt(jnp.mean(x * x, axis=-1, keepdims=True) + EPS)
    return y * g


def _norm_mod_kernel(x_ref, g_ref, sh_ref, sc_ref, o_ref):
    y = _rms(x_ref[...], g_ref[...])
    o_ref[...] = (y * (1 + sc_ref[...]) + sh_ref[...]).astype(o_ref.dtype)


def _norm_mod(x, g, mod4, layer, which_shift):
    return pl.pallas_call(
        _norm_mod_kernel,
        grid=(T_ALL // BM_NORM,),
        in_specs=[pl.BlockSpec((BM_NORM, D_MODEL), lambda i: (i, 0)),
                  pl.BlockSpec((None, 1, D_MODEL), lambda i: (layer, 0, 0)),
                  _mod_spec(layer, which_shift, BM_NORM),
                  _mod_spec(layer, which_shift + 1, BM_NORM)],
        out_specs=pl.BlockSpec((BM_NORM, D_MODEL), lambda i: (i, 0)),
        out_shape=jax.ShapeDtypeStruct((T_ALL, D_MODEL), BF16),
        compiler_params=_params(("arbitrary",), _vmem_limit([BM_NORM * D_MODEL * 4, BM_NORM * D_MODEL * 2],
                                                            temp_bytes=2 * BM_NORM * D_MODEL * 4)),
        name="norm_mod",
    )(x, g.reshape(DEPTH, 1, D_MODEL), mod4, mod4)


def _norm_kernel(x_ref, g_ref, o_ref):
    o_ref[...] = _rms(x_ref[...], g_ref[...])


def _final_norm(x, g, row0, rows):
    blk0 = row0 // BM_NORM
    return pl.pallas_call(
        _norm_kernel,
        grid=(rows // BM_NORM,),
        in_specs=[pl.BlockSpec((BM_NORM, D_MODEL), lambda i: (blk0 + i, 0)),
                  pl.BlockSpec((1, D_MODEL), lambda i: (0, 0))],
        out_specs=pl.BlockSpec((BM_NORM, D_MODEL), lambda i: (i, 0)),
        out_shape=jax.ShapeDtypeStruct((rows, D_MODEL), F32),
        compiler_params=_params(("arbitrary",), _vmem_limit([BM_NORM * D_MODEL * 4] * 2,
                                                            temp_bytes=2 * BM_NORM * D_MODEL * 4)),
        name="final_norm",
    )(x, g.reshape(1, D_MODEL))


def _in_proj_kernel(h_ref, w_ref, z_ref, kv_ref):
    acc = jnp.dot(h_ref[...], w_ref[...], preferred_element_type=F32)
    z_ref[...] = acc.astype(z_ref.dtype)

    @pl.when(pl.program_id(1) == KV_BLOCK)
    def _():
        kv_ref[...] = acc


def _in_proj(h, w_in, layer):
    return pl.pallas_call(
        _in_proj_kernel,
        grid=(T_ALL // BM, IN_COLS // BN),
        in_specs=[pl.BlockSpec((BM, D_MODEL), lambda i, j: (i, 0)),
                  pl.BlockSpec((None, D_MODEL, BN), lambda i, j: (layer, 0, j))],
        out_specs=[pl.BlockSpec((BM, BN), lambda i, j: (i, j)),
                   pl.BlockSpec((BM, BN), lambda i, j: (i, 0))],
        out_shape=[jax.ShapeDtypeStruct((T_ALL, IN_COLS), BF16),
                   jax.ShapeDtypeStruct((T_ALL, BN), F32)],
        compiler_params=_params(("arbitrary", "arbitrary"),
                                _vmem_limit([BM * D_MODEL * 2, D_MODEL * BN * 2, BM * BN * 2, BM * BN * 4],
                                            temp_bytes=2 * BM * BN * 4)),
        name="in_proj",
    )(h, w_in)


def _conv_kernel(cb_ref, cc_ref, cx_ref, w_ref, b_ref, o_ref):
    i = pl.program_id(0)
    seq = jnp.where(i < T_CTX // BM, SEQ, DEC_SEQ)
    u = cc_ref[...].astype(F32) * cx_ref[...].astype(F32)
    pos = lax.broadcasted_iota(jnp.int32, u.shape, 0) & (seq - 1)
    prev = jnp.where(pos == 0, 0.0, pltpu.roll(u, 1, 0))
    nxt = jnp.where(pos == seq - 1, 0.0, pltpu.roll(u, BM - 1, 0))
    w = w_ref[...]
    conv = w[0:1] * prev + w[1:2] * u + w[2:3] * nxt + b_ref[...]
    o_ref[...] = (cb_ref[...].astype(F32) * conv).astype(o_ref.dtype)


def _conv_branch(z, conv_w, conv_b, layer):
    nb = CONV_DIM // BN_CONV
    zspec = lambda off: pl.BlockSpec((BM, BN_CONV), lambda i, j: (i, off // BN_CONV + j))
    return pl.pallas_call(
        _conv_kernel,
        grid=(T_ALL // BM, nb),
        in_specs=[zspec(OFF_CB), zspec(OFF_CC), zspec(OFF_CX),
                  pl.BlockSpec((None, 3, BN_CONV), lambda i, j: (layer, 0, j)),
                  pl.BlockSpec((None, 1, BN_CONV), lambda i, j: (layer, 0, j))],
        out_specs=pl.BlockSpec((BM, BN_CONV), lambda i, j: (i, j)),
        out_shape=jax.ShapeDtypeStruct((T_ALL, CONV_DIM), BF16),
        compiler_params=_params(("arbitrary", "arbitrary"),
                                _vmem_limit([BM * BN_CONV * 2] * 4, temp_bytes=6 * BM * BN_CONV * 4)),
        name="conv_branch",
    )(z, z, z, conv_w, conv_b.reshape(DEPTH, 1, CONV_DIM))


def _sink_column(sink_ref, layer, kvh, rows_per_head):
    row = lax.broadcasted_iota(jnp.int32, (GROUP * rows_per_head, 1), 0)
    col = jnp.full(row.shape, sink_ref[layer, kvh * GROUP], F32)
    for g in range(1, GROUP):
        col = jnp.where(row >= g * rows_per_head, sink_ref[layer, kvh * GROUP + g], col)
    return col


def _stack_heads(q):
    return jnp.concatenate([q[:, g * HEAD_DIM:(g + 1) * HEAD_DIM] for g in range(GROUP)], axis=0)


def _qk(q, k):
    return lax.dot_general(q, k, (((1,), (1,)), ((), ())), preferred_element_type=F32)


def _ctx_attn_kernel(layer, sink_ref, q_ref, k_ref, v_ref, o_ref):
    kvh = pl.program_id(1)
    q4 = _stack_heads(q_ref[...])
    s = _qk(q4, k_ref[...]) * HEAD_DIM ** -0.5
    sink = _sink_column(sink_ref, layer, kvh, SEQ)
    m = jnp.maximum(jnp.max(s, axis=-1, keepdims=True), sink)
    e = jnp.exp(s - m)
    p = e / (jnp.sum(e, axis=-1, keepdims=True) + jnp.exp(sink - m))
    o = jnp.dot(p.astype(BF16), v_ref[...], preferred_element_type=F32)
    for g in range(GROUP):
        o_ref[:, g * HEAD_DIM:(g + 1) * HEAD_DIM] = o[g * SEQ:(g + 1) * SEQ].astype(o_ref.dtype)


def _ctx_attention(z, attn_sink, layer):
    qw = GROUP * HEAD_DIM
    return pl.pallas_call(
        functools.partial(_ctx_attn_kernel, layer),
        grid=(BATCH, N_KV_HEADS),
        in_specs=[pl.BlockSpec(memory_space=pltpu.SMEM),
                  pl.BlockSpec((SEQ, qw), lambda b, h: (b, OFF_Q // qw + h)),
                  pl.BlockSpec((SEQ, HEAD_DIM), lambda b, h: (b, OFF_K // HEAD_DIM + h)),
                  pl.BlockSpec((SEQ, HEAD_DIM), lambda b, h: (b, OFF_V // HEAD_DIM + h))],
        out_specs=pl.BlockSpec((SEQ, qw), lambda b, h: (b, h)),
        out_shape=jax.ShapeDtypeStruct((T_ALL, ATTN_DIM), BF16),
        compiler_params=_params(("arbitrary", "arbitrary"),
                                _vmem_limit([SEQ * qw * 2] * 2, temp_bytes=6 * GROUP * SEQ * SEQ * 4)),
        name="ctx_attention",
    )(attn_sink, z, z, z)


def _rope_tables():
    pos = np.arange(DEC_SEQ)
    row_id = jnp.asarray(pos // GRID_W, F32)
    col_id = jnp.asarray(pos % GRID_W, F32)
    half = HEAD_DIM // 2
    freqs = ROPE_BASE ** (-jnp.arange(0, half, 2, dtype=F32) / half)
    ang_r = row_id[:, None] * freqs[None, :]
    ang_c = col_id[:, None] * freqs[None, :]
    cos = jnp.concatenate([jnp.cos(ang_r), jnp.cos(ang_r), jnp.cos(ang_c), jnp.cos(ang_c)], axis=-1)
    sin = jnp.concatenate([-jnp.sin(ang_r), jnp.sin(ang_r), -jnp.sin(ang_c), jnp.sin(ang_c)], axis=-1)
    return cos, sin


def _rope_kernel(x_ref, cos_ref, sin_ref, o_ref):
    cos = cos_ref[...]
    sin = sin_ref[...]
    quarter = HEAD_DIM // 4
    lane = lax.broadcasted_iota(jnp.int32, cos.shape, 1)
    first = (lane % (2 * quarter)) < quarter
    for g in range(GROUP):
        x = x_ref[:, g * HEAD_DIM:(g + 1) * HEAD_DIM].astype(F32)
        partner = jnp.where(first, pltpu.roll(x, HEAD_DIM - quarter, 1), pltpu.roll(x, quarter, 1))
        o_ref[:, g * HEAD_DIM:(g + 1) * HEAD_DIM] = (x * cos + partner * sin).astype(o_ref.dtype)


def _rope_qk(z, cos, sin):
    w = GROUP * HEAD_DIM
    nb = (ATTN_DIM + KV_DIM) // w
    tab = pl.BlockSpec((DEC_SEQ, HEAD_DIM), lambda b, j: (0, 0))
    return pl.pallas_call(
        _rope_kernel,
        grid=(DEC_BATCH, nb),
        in_specs=[pl.BlockSpec((DEC_SEQ, w), lambda b, j: (T_CTX // DEC_SEQ + b, OFF_Q // w + j)), tab, tab],
        out_specs=pl.BlockSpec((DEC_SEQ, w), lambda b, j: (b, j)),
        out_shape=jax.ShapeDtypeStruct((T_LAT, ATTN_DIM + KV_DIM), BF16),
        compiler_params=_params(("arbitrary", "arbitrary"),
                                _vmem_limit([DEC_SEQ * w * 2] * 2 + [DEC_SEQ * HEAD_DIM * 4] * 2,
                                            temp_bytes=6 * DEC_SEQ * HEAD_DIM * 4)),
        name="rope_qk",
    )(z, cos, sin)


def _lat_attn_kernel(layer, sink_ref, q_ref, k_ref, v_ref, kc_ref, vc_ref, _, o_ref):
    kvh = pl.program_id(1)
    nblk = DEC_SEQ // BLOCK
    scale = HEAD_DIM ** -0.5
    sink = _sink_column(sink_ref, layer, kvh, BLOCK)
    k_ctx = kc_ref[...].astype(BF16)
    v_ctx = vc_ref[...].astype(BF16)
    for n in range(nblk):
        lo = max(n - 1, 0) * BLOCK
        hi = min(n + 2, nblk) * BLOCK
        q4 = _stack_heads(q_ref[n * BLOCK:(n + 1) * BLOCK, :])
        s_loc = _qk(q4, k_ref[lo:hi, :]) * scale
        qpos = n * BLOCK + lax.broadcasted_iota(jnp.int32, s_loc.shape, 0) % BLOCK
        kpos = lo + lax.broadcasted_iota(jnp.int32, s_loc.shape, 1)
        s_loc = jnp.where(jnp.abs(kpos - qpos) <= WINDOW, s_loc, -jnp.inf)
        s_ctx = _qk(q4, k_ctx) * scale
        m = jnp.maximum(jnp.maximum(jnp.max(s_loc, axis=-1, keepdims=True),
                                    jnp.max(s_ctx, axis=-1, keepdims=True)), sink)
        e_loc = jnp.exp(s_loc - m)
        e_ctx = jnp.exp(s_ctx - m)
        den = (jnp.sum(e_loc, axis=-1, keepdims=True) + jnp.sum(e_ctx, axis=-1, keepdims=True)
               + jnp.exp(sink - m))
        o = (jnp.dot((e_loc / den).astype(BF16), v_ref[lo:hi, :], preferred_element_type=F32)
             + jnp.dot((e_ctx / den).astype(BF16), v_ctx, preferred_element_type=F32))
        for g in range(GROUP):
            o_ref[n * BLOCK:(n + 1) * BLOCK, g * HEAD_DIM:(g + 1) * HEAD_DIM] = (
                o[g * BLOCK:(g + 1) * BLOCK].astype(o_ref.dtype))


def _lat_attention(qk_rot, z, cache_k, cache_v, attn_sink, layer, y_ctx):
    qw = GROUP * HEAD_DIM
    ck = cache_k.reshape(DEC_BATCH, DEPTH, PAST_LEN, KV_DIM)
    cv = cache_v.reshape(DEC_BATCH, DEPTH, PAST_LEN, KV_DIM)
    cspec = pl.BlockSpec((None, None, PAST_LEN, HEAD_DIM), lambda b, h: (b, layer, 0, h))
    return pl.pallas_call(
        functools.partial(_lat_attn_kernel, layer),
        grid=(DEC_BATCH, N_KV_HEADS),
        in_specs=[pl.BlockSpec(memory_space=pltpu.SMEM),
                  pl.BlockSpec((DEC_SEQ, qw), lambda b, h: (b, h)),
                  pl.BlockSpec((DEC_SEQ, HEAD_DIM), lambda b, h: (b, ATTN_DIM // HEAD_DIM + h)),
                  pl.BlockSpec((DEC_SEQ, HEAD_DIM), lambda b, h: (T_CTX // DEC_SEQ + b, OFF_V // HEAD_DIM + h)),
                  cspec, cspec, pl.BlockSpec(memory_space=pl.ANY)],
        out_specs=pl.BlockSpec((DEC_SEQ, qw), lambda b, h: (T_CTX // DEC_SEQ + b, h)),
        out_shape=jax.ShapeDtypeStruct((T_ALL, ATTN_DIM), BF16),
        input_output_aliases={6: 0},
        compiler_params=_params(("arbitrary", "arbitrary"),
                                _vmem_limit([DEC_SEQ * qw * 2] * 2 + [PAST_LEN * HEAD_DIM * 4] * 2,
                                            temp_bytes=16 << 20)),
        name="lat_attention",
    )(attn_sink, qk_rot, qk_rot, z, ck, cv, y_ctx)


def _log_sigmoid(x):
    return jnp.minimum(x, 0.0) - jnp.log(1.0 + jnp.exp(-jnp.abs(x)))


def _head_norm(o, g):
    mu = jnp.mean(o, axis=-1, keepdims=True)
    d = o - mu
    var = jnp.mean(d * d, axis=-1, keepdims=True)
    return d * lax.rsqrt(var + EPS) * g


def _silu(x):
    return x * jax.nn.sigmoid(x)


def _kt_v(k, v):
    return lax.dot_general(k, v, (((0,), (0,)), ((), ())), preferred_element_type=F32)


def _retention_kernel(seq_len, heads, has_state, n_alias, *refs):
    df_ref, db_ref, rq_ref, rk_ref, rv_ref, gf_ref, gb_ref, gn_ref = refs[:8]
    rest = refs[8:]
    if has_state:
        s0f_ref, s0b_ref = rest[:2]
        rest = rest[2:]
    rest = rest[n_alias:]
    if has_state:
        y_ref, of_scr, ob_scr = rest
    else:
        y_ref, sf_ref, sb_ref, of_scr, ob_scr = rest
    n = seq_len // RET_CHUNK
    c = RET_CHUNK
    ii = lax.broadcasted_iota(jnp.int32, (c, c), 0)
    jj = lax.broadcasted_iota(jnp.int32, (c, c), 1)
    row = lax.broadcasted_iota(jnp.int32, (c, RET_DV), 0).astype(F32)

    def direction(hd, lg, rel, q_pow, k_pow, s, order, o_scr):
        cols = slice(hd * RET_DV, (hd + 1) * RET_DV)
        keep = rel >= 0
        dmat = jnp.where(keep, jnp.exp(jnp.where(keep, rel, 0).astype(F32) * lg[:, :c]), 0.0)
        q_dec = jnp.exp(q_pow * lg)
        k_dec = jnp.exp(k_pow * lg)
        chunk_dec = jnp.exp(c * lg)
        for t in order:
            rows = slice(t * c, (t + 1) * c)
            qc = rq_ref[rows, cols]
            ks = rk_ref[rows, cols].astype(F32) * RET_DK ** -0.5
            vc = rv_ref[rows, cols]
            inner = _qk(qc, ks.astype(BF16)) * dmat
            o = jnp.dot(inner.astype(BF16), vc, preferred_element_type=F32)
            kv = _kt_v((ks * k_dec).astype(BF16), vc)
            if s is None:
                s = kv
            else:
                o = o + jnp.dot(qc, s.astype(BF16), preferred_element_type=F32) * q_dec
                s = chunk_dec * s + kv
            o_scr[rows, cols] = o
        return s

    for hd in range(heads):
        lg_f = _log_sigmoid(df_ref[hd])
        lg_b = _log_sigmoid(db_ref[hd])
        s_f = direction(hd, lg_f, ii - jj, row + 1.0, c - 1.0 - row,
                        s0f_ref[hd] if has_state else None, range(n), of_scr)
        s_b = direction(hd, lg_b, jj - ii, c - row, row,
                        s0b_ref[hd] if has_state else None, range(n - 1, -1, -1), ob_scr)
        if not has_state:
            sf_ref[hd] = s_f
            sb_ref[hd] = s_b
        cols = slice(hd * RET_DV, (hd + 1) * RET_DV)
        gn = gn_ref[hd]
        for t in range(n):
            rows = slice(t * c, (t + 1) * c)
            y = (_silu(gf_ref[rows, cols].astype(F32)) * _head_norm(of_scr[rows, cols], gn)
                 + _silu(gb_ref[rows, cols].astype(F32)) * _head_norm(ob_scr[rows, cols], gn))
            y_ref[rows, cols] = y.astype(y_ref.dtype)


def _retention(z, dec_f, dec_b, gn_g, layer, seq_len, row0, n_seq, heads, in_place=(), states=None):
    has_state = states is not None
    blk0 = row0 // seq_len
    w = heads * RET_DV
    zspec = lambda off: pl.BlockSpec((seq_len, w), lambda b, h: (blk0 + b, off // w + h))
    pspec = pl.BlockSpec((None, heads, 1, RET_DV), lambda b, h: (layer, h, 0, 0))
    sspec = pl.BlockSpec((None, None, heads, RET_DK, RET_DV), lambda b, h: (b, layer, h, 0, 0))
    in_specs = [pspec, pspec, zspec(OFF_RQ), zspec(OFF_RK), zspec(OFF_RV), zspec(OFF_RGF), zspec(OFF_RGB), pspec]
    args = [dec_f, dec_b, z, z, z, z, z, gn_g.reshape(DEPTH, RET_HEADS, 1, RET_DV)]
    if has_state:
        in_specs += [sspec, sspec]
        args += list(states)
    aliases = {}
    for k, prev in enumerate(in_place):
        if prev is not None:
            aliases[len(args)] = k
            in_specs.append(pl.BlockSpec(memory_space=pl.ANY))
            args.append(prev)
    y_spec = pl.BlockSpec((seq_len, w), lambda b, h: (blk0 + b, h))
    y_shape = jax.ShapeDtypeStruct((T_ALL, RET_V_DIM), BF16)
    st_bytes = heads * RET_DK * RET_DV * 4
    if has_state:
        out_specs, out_shape = y_spec, y_shape
    else:
        oshape = jax.ShapeDtypeStruct((n_seq, DEPTH, RET_HEADS, RET_DK, RET_DV), F32)
        out_specs, out_shape = [y_spec, sspec, sspec], [y_shape, oshape, oshape]
    return pl.pallas_call(
        functools.partial(_retention_kernel, seq_len, heads, has_state, len(aliases)),
        grid=(n_seq, RET_HEADS // heads),
        in_specs=in_specs,
        out_specs=out_specs,
        out_shape=out_shape,
        input_output_aliases=aliases,
        scratch_shapes=[pltpu.VMEM((seq_len, w), F32), pltpu.VMEM((seq_len, w), F32)],
        compiler_params=_params(("arbitrary", "arbitrary"),
                                _vmem_limit([seq_len * w * 2] * 6 + [st_bytes] * 2,
                                            scratch_bytes=2 * seq_len * w * 4, temp_bytes=8 << 20)),
        name="retention",
    )(*args)


def _merge_kernel(yc_ref, ya_ref, yr_ref, wc_ref, wa_ref, wr_ref, ga_ref, gb_ref, gr_ref, o_ref):
    def term(y_ref, w_ref, gate_ref):
        proj = jnp.dot(y_ref[...], w_ref[...], preferred_element_type=F32)
        return jax.nn.sigmoid(gate_ref[...].astype(F32)) * proj

    merged = term(yc_ref, wc_ref, ga_ref) + term(ya_ref, wa_ref, gb_ref) + term(yr_ref, wr_ref, gr_ref)
    o_ref[...] = merged.astype(o_ref.dtype)


def _merge(y_conv, y_attn, y_ret, w_conv_out, w_attn_out, w_ret_out, z, layer):
    bn = BN_MERGE
    yspec = pl.BlockSpec((BM, CONV_DIM), lambda i, j: (i, 0))
    wspec = pl.BlockSpec((None, CONV_DIM, bn), lambda i, j: (layer, 0, j))
    gspec = lambda off: pl.BlockSpec((BM, bn), lambda i, j: (i, off // bn + j))
    return pl.pallas_call(
        _merge_kernel,
        grid=(T_ALL // BM, D_MODEL // bn),
        in_specs=[yspec, yspec, yspec, wspec, wspec, wspec, gspec(OFF_GA), gspec(OFF_GB), gspec(OFF_GR)],
        out_specs=pl.BlockSpec((BM, bn), lambda i, j: (i, j)),
        out_shape=jax.ShapeDtypeStruct((T_ALL, D_MODEL), BF16),
        compiler_params=_params(("arbitrary", "arbitrary"),
                                _vmem_limit([BM * CONV_DIM * 2] * 3 + [CONV_DIM * bn * 2] * 3 + [BM * bn * 2] * 4,
                                            temp_bytes=3 * BM * bn * 4)),
        name="merge",
    )(y_conv, y_attn, y_ret, w_conv_out, w_attn_out, w_ret_out, z, z, z)


def _wo_kernel(m_ref, w_ref, x_ref, g_ref, o_ref):
    acc = jnp.dot(m_ref[...], w_ref[...], preferred_element_type=F32)
    o_ref[...] = x_ref[...] + g_ref[...] * acc


def _out_proj(merged, w_o, x, mod4, layer):
    return pl.pallas_call(
        _wo_kernel,
        grid=(T_ALL // BM, D_MODEL // BN),
        in_specs=[pl.BlockSpec((BM, D_MODEL), lambda i, j: (i, 0)),
                  pl.BlockSpec((None, D_MODEL, BN), lambda i, j: (layer, 0, j)),
                  pl.BlockSpec((BM, BN), lambda i, j: (i, j)),
                  _mod_spec(layer, 2, BM, BN, with_j=True)],
        out_specs=pl.BlockSpec((BM, BN), lambda i, j: (i, j)),
        out_shape=jax.ShapeDtypeStruct((T_ALL, D_MODEL), F32),
        compiler_params=_params(("arbitrary", "arbitrary"),
                                _vmem_limit([BM * D_MODEL * 2, D_MODEL * BN * 2, BM * BN * 4, BM * BN * 4],
                                            temp_bytes=2 * BM * BN * 4)),
        name="out_proj",
    )(merged, w_o, x, mod4)


def _ff1_kernel(h_ref, w_ref, b_ref, o_ref):
    acc = jnp.dot(h_ref[...], w_ref[...], preferred_element_type=F32) + b_ref[...]
    r = jnp.maximum(acc, 0.0)
    o_ref[...] = (r * r).astype(o_ref.dtype)


def _ff1(h, w_ff1, b_ff1, layer):
    return pl.pallas_call(
        _ff1_kernel,
        grid=(T_ALL // BM, D_FF // BN),
        in_specs=[pl.BlockSpec((BM, D_MODEL), lambda i, j: (i, 0)),
                  pl.BlockSpec((None, D_MODEL, BN), lambda i, j: (layer, 0, j)),
                  pl.BlockSpec((None, 1, BN), lambda i, j: (layer, 0, j))],
        out_specs=pl.BlockSpec((BM, BN), lambda i, j: (i, j)),
        out_shape=jax.ShapeDtypeStruct((T_ALL, D_FF), BF16),
        compiler_params=_params(("arbitrary", "arbitrary"),
                                _vmem_limit([BM * D_MODEL * 2, D_MODEL * BN * 2, BM * BN * 2],
                                            temp_bytes=2 * BM * BN * 4)),
        name="ff1",
    )(h, w_ff1, b_ff1.reshape(DEPTH, 1, D_FF))


def _ff2_kernel(f_ref, w_ref, b_ref, x_ref, g_ref, o_ref, acc_ref):
    k = pl.program_id(2)
    part = jnp.dot(f_ref[...], w_ref[...], preferred_element_type=F32)

    @pl.when(k == 0)
    def _():
        acc_ref[...] = part

    @pl.when(k > 0)
    def _():
        acc_ref[...] += part

    @pl.when(k == pl.num_programs(2) - 1)
    def _():
        o_ref[...] = x_ref[...] + g_ref[...] * (acc_ref[...] + b_ref[...])


def _ff2(f, w_ff2, b_ff2, x, mod4, layer):
    bk = BK_FF2
    per = D_MODEL // BN
    return pl.pallas_call(
        _ff2_kernel,
        grid=(T_ALL // BM, D_MODEL // BN, D_FF // bk),
        in_specs=[pl.BlockSpec((BM, bk), lambda i, j, k: (i, k)),
                  pl.BlockSpec((None, bk, BN), lambda i, j, k: (layer, k, j)),
                  pl.BlockSpec((None, 1, BN), lambda i, j, k: (layer, 0, j)),
                  pl.BlockSpec((BM, BN), lambda i, j, k: (i, j)),
                  pl.BlockSpec((None, None, 1, BN), lambda i, j, k: (layer, _cond_row(i, BM), 0, 5 * per + j))],
        out_specs=pl.BlockSpec((BM, BN), lambda i, j, k: (i, j)),
        out_shape=jax.ShapeDtypeStruct((T_ALL, D_MODEL), F32),
        scratch_shapes=[pltpu.VMEM((BM, BN), F32)],
        compiler_params=_params(("arbitrary", "arbitrary", "arbitrary"),
                                _vmem_limit([BM * bk * 2, bk * BN * 2, BM * BN * 4, BM * BN * 4],
                                            scratch_bytes=BM * BN * 4, temp_bytes=2 * BM * BN * 4)),
        name="ff2",
    )(f, w_ff2, b_ff2.reshape(DEPTH, 1, D_MODEL), x, mod4)


def _ret_param(p):
    return jnp.broadcast_to(p.astype(F32)[:, :, None, None], (DEPTH, RET_HEADS, 1, RET_DV))


def kernel(x_prompt, x_sample, c, cache_k, cache_v, state_ret_f, state_ret_b, c_ctx, w_mod, b_mod, norm1_g, w_in, conv_w, conv_b, attn_sink, ret_decay_f, ret_decay_b, ret_gn_g, w_conv_out, w_attn_out, w_ret_out, w_o, norm2_g, w_ff1, b_ff1, w_ff2, b_ff2, final_g):
    x = jnp.concatenate([x_prompt.reshape(T_CTX, D_MODEL), x_sample.reshape(T_LAT, D_MODEL)], axis=0)
    cond = jnp.concatenate([c_ctx[None, :], c, jnp.zeros((COND_PAD - N_COND, D_MODEL), F32)], axis=0)
    mod4 = _adaln(cond, w_mod, b_mod).reshape(DEPTH, COND_PAD, 1, N_MOD * D_MODEL)

    w_in_b = w_in.astype(BF16)
    w_conv_b = w_conv_out.astype(BF16)
    w_attn_b = w_attn_out.astype(BF16)
    w_ret_b = w_ret_out.astype(BF16)
    w_o_b = w_o.astype(BF16)
    w_ff1_b = w_ff1.astype(BF16)
    w_ff2_b = w_ff2.astype(BF16)
    dec_f = _ret_param(ret_decay_f)
    dec_b = _ret_param(ret_decay_b)
    cos, sin = _rope_tables()

    ks, vs = [], []
    new_states = ()
    for l in range(DEPTH):
        h = _norm_mod(x, norm1_g, mod4, l, 0)
        z, kv = _in_proj(h, w_in_b, l)
        ks.append(kv[:T_CTX, :KV_DIM].reshape(BATCH, SEQ, N_KV_HEADS, HEAD_DIM))
        vs.append(kv[:T_CTX, KV_DIM:].reshape(BATCH, SEQ, N_KV_HEADS, HEAD_DIM))

        y_conv = _conv_branch(z, conv_w, conv_b, l)
        y_attn = _lat_attention(_rope_qk(z, cos, sin), z, cache_k, cache_v, attn_sink, l,
                                _ctx_attention(z, attn_sink, l))
        y_ret, *new_states = _retention(z, dec_f, dec_b, ret_gn_g, l, SEQ, 0, BATCH, RET_HEADS_PER_STEP_CTX,
                                        in_place=(None,) + tuple(new_states) if new_states else ())
        y_ret = _retention(z, dec_f, dec_b, ret_gn_g, l, DEC_SEQ, T_CTX, DEC_BATCH, 1,
                           in_place=(y_ret,), states=(state_ret_f, state_ret_b))

        merged = _merge(y_conv, y_attn, y_ret, w_conv_b, w_attn_b, w_ret_b, z, l)
        x = _out_proj(merged, w_o_b, x, mod4, l)
        h = _norm_mod(x, norm2_g, mod4, l, 3)
        f = _ff1(h, w_ff1_b, b_ff1, l)
        x = _ff2(f, w_ff2_b, b_ff2, x, mod4, l)

    y_prompt = _final_norm(x, final_g, 0, T_CTX).reshape(BATCH, SEQ, D_MODEL)
    y_sample = _final_norm(x, final_g, T_CTX, T_LAT).reshape(DEC_BATCH, DEC_SEQ, D_MODEL)
    return (y_prompt, y_sample, jnp.stack(ks, axis=1), jnp.stack(vs, axis=1), new_states[0], new_states[1])
```

```python
import functools

import numpy as np
import jax
import jax.numpy as jnp
from jax import lax
from jax.experimental import pallas as pl
from jax.experimental.pallas import tpu as pltpu

F32 = jnp.float32
BF16 = jnp.bfloat16

D_MODEL = 4096
BATCH = 16
SEQ = 256
DEPTH = 2
DEC_BATCH = 8
DEC_SEQ = 1024
PAST_LEN = 256
GRID_W = 64
EPS = 1e-6
N_MOD = 6
CONV_DIM = 2048
N_Q_HEADS = 16
N_KV_HEADS = 4
GROUP = N_Q_HEADS // N_KV_HEADS
HEAD_DIM = 128
ATTN_DIM = N_Q_HEADS * HEAD_DIM
KV_DIM = N_KV_HEADS * HEAD_DIM
WINDOW = 128
BLOCK = 128
ROPE_BASE = 10000.0
RET_HEADS = 8
RET_DK = 256
RET_DV = 256
RET_V_DIM = RET_HEADS * RET_DV
RET_CHUNK = 128
D_FF = 4 * D_MODEL

T_CTX = BATCH * SEQ
T_LAT = DEC_BATCH * DEC_SEQ
T_ALL = T_CTX + T_LAT
N_COND = 1 + DEC_BATCH
COND_PAD = 16

OFF_CB = 0
OFF_CC = OFF_CB + CONV_DIM
OFF_CX = OFF_CC + CONV_DIM
OFF_Q = OFF_CX + CONV_DIM
OFF_K = OFF_Q + ATTN_DIM
OFF_V = OFF_K + KV_DIM
OFF_RQ = OFF_V + KV_DIM
OFF_RK = OFF_RQ + RET_HEADS * RET_DK
OFF_RV = OFF_RK + RET_HEADS * RET_DK
OFF_RGF = OFF_RV + RET_V_DIM
OFF_RGB = OFF_RGF + RET_V_DIM
OFF_GA = OFF_RGB + RET_V_DIM
OFF_GB = OFF_GA + D_MODEL
OFF_GR = OFF_GB + D_MODEL
IN_COLS = OFF_GR + D_MODEL

V7X_VMEM_LIMIT_CAP = 60000 * 1024
BM = 1024
BN = 1024
BK_FF2 = 2048
BM_NORM = 256
BN_MERGE = 512
BN_ADALN = 512
BN_CONV = 512
RET_HEADS_PER_STEP_CTX = 4
KC_WS = 512
BN_KV = 512
assert OFF_K % BN_KV == 0 and OFF_V == OFF_K + KV_DIM
assert BM == DEC_SEQ and T_CTX % BM == 0 and BM % SEQ == 0


def _vmem_limit(block_bytes, scratch_bytes=0, temp_bytes=0):
    need = 2 * sum(block_bytes) + scratch_bytes + temp_bytes + (2 << 20)
    return int(min(need, V7X_VMEM_LIMIT_CAP))


def _params(semantics, vmem):
    return pltpu.CompilerParams(dimension_semantics=semantics, vmem_limit_bytes=vmem)


def _cond_row(i, bm):
    n_ctx = T_CTX // bm
    per_seq = DEC_SEQ // bm
    return jnp.where(i < n_ctx, 0, 1 + (i - n_ctx) // per_seq)


def _mod_spec(layer, which, bm, bn=D_MODEL, with_j=False):
    per = D_MODEL // bn
    if with_j:
        return pl.BlockSpec((None, None, 1, bn),
                            lambda i, j, *_: (layer, _cond_row(i, bm), 0, which * per + j))
    return pl.BlockSpec((None, None, 1, bn), lambda i, *_: (layer, _cond_row(i, bm), 0, which * per))


def _adaln_kernel(c_ref, w_ref, b_ref, o_ref):
    s = c_ref[...]
    s = s * jax.nn.sigmoid(s)
    acc = jnp.dot(s.astype(BF16), w_ref[...].astype(BF16), preferred_element_type=F32)
    o_ref[...] = acc + b_ref[...]


def _adaln(cond, w_mod, b_mod):
    n = N_MOD * D_MODEL
    return pl.pallas_call(
        _adaln_kernel,
        grid=(DEPTH, n // BN_ADALN),
        in_specs=[pl.BlockSpec((COND_PAD, D_MODEL), lambda l, j: (0, 0)),
                  pl.BlockSpec((None, D_MODEL, BN_ADALN), lambda l, j: (l, 0, j)),
                  pl.BlockSpec((None, 1, BN_ADALN), lambda l, j: (l, 0, j))],
        out_specs=pl.BlockSpec((None, COND_PAD, BN_ADALN), lambda l, j: (l, 0, j)),
        out_shape=jax.ShapeDtypeStruct((DEPTH, COND_PAD, n), F32),
        compiler_params=_params(("arbitrary", "arbitrary"),
                                _vmem_limit([D_MODEL * BN_ADALN * 4], temp_bytes=D_MODEL * BN_ADALN * 2)),
        name="adaln",
    )(cond, w_mod, b_mod.reshape(DEPTH, 1, n))


def _rms(x, g):
    y = x * lax.rsqrt(jnp.mean(x * x, axis=-1, keepdims=True) + EPS)
    return y * g


def _norm_mod_kernel(x_ref, g_ref, sh_ref, sc_ref, o_ref):
    y = _rms(x_ref[...], g_ref[...])
    o_ref[...] = (y * (1 + sc_ref[...]) + sh_ref[...]).astype(o_ref.dtype)


def _norm_mod(x, g, mod4, layer, which_shift):
    return pl.pallas_call(
        _norm_mod_kernel,
        grid=(T_ALL // BM_NORM,),
        in_specs=[pl.BlockSpec((BM_NORM, D_MODEL), lambda i: (i, 0)),
                  pl.BlockSpec((None, 1, D_MODEL), lambda i: (layer, 0, 0)),
                  _mod_spec(layer, which_shift, BM_NORM),
                  _mod_spec(layer, which_shift + 1, BM_NORM)],
        out_specs=pl.BlockSpec((BM_NORM, D_MODEL), lambda i: (i, 0)),
        out_shape=jax.ShapeDtypeStruct((T_ALL, D_MODEL), BF16),
        compiler_params=_params(("arbitrary",), _vmem_limit([BM_NORM * D_MODEL * 4, BM_NORM * D_MODEL * 2],
                                                            temp_bytes=2 * BM_NORM * D_MODEL * 4)),
        name="norm_mod",
    )(x, g.reshape(DEPTH, 1, D_MODEL), mod4, mod4)


def _norm_kernel(x_ref, g_ref, o_ref):
    o_ref[...] = _rms(x_ref[...], g_ref[...])


def _final_norm(x, g, row0, rows):
    blk0 = row0 // BM_NORM
    return pl.pallas_call(
        _norm_kernel,
        grid=(rows // BM_NORM,),
        in_specs=[pl.BlockSpec((BM_NORM, D_MODEL), lambda i: (blk0 + i, 0)),
                  pl.BlockSpec((1, D_MODEL), lambda i: (0, 0))],
        out_specs=pl.BlockSpec((BM_NORM, D_MODEL), lambda i: (i, 0)),
        out_shape=jax.ShapeDtypeStruct((rows, D_MODEL), F32),
        compiler_params=_params(("arbitrary",), _vmem_limit([BM_NORM * D_MODEL * 4] * 2,
                                                            temp_bytes=2 * BM_NORM * D_MODEL * 4)),
        name="final_norm",
    )(x, g.reshape(1, D_MODEL))


def _ws_matmul_kernel(epilogue, a_ref, wc_ref, *rest):
    *extra, o_ref, wbf_scr = rest
    jj = pl.program_id(0)
    i = pl.program_id(1)
    n_col = pl.num_programs(0) - 1
    n_chunk = wbf_scr.shape[1] // KC_WS

    @pl.when(jnp.logical_and(jj < n_col, i < n_chunk))
    def _():
        r0 = pl.multiple_of(i * KC_WS, KC_WS)
        wbf_scr[jj % 2, pl.ds(r0, KC_WS), :] = wc_ref[...].astype(BF16)

    @pl.when(jj > 0)
    def _():
        acc = jnp.dot(a_ref[...], wbf_scr[(jj - 1) % 2], preferred_element_type=F32)
        o_ref[...] = epilogue(acc, *extra).astype(o_ref.dtype)


def _ws_matmul(a, w, layer, epilogue, extra=(), name="ws_matmul"):
    k, n = w.shape[1:]
    n_col = n // BN
    n_row = T_ALL // BM
    n_chunk = k // KC_WS
    assert n_row >= n_chunk and k % KC_WS == 0 and n % BN == 0
    row = lambda jj, i: jnp.where(jj == 0, 0, i)
    col = lambda jj: jnp.maximum(jj - 1, 0)
    return pl.pallas_call(
        functools.partial(_ws_matmul_kernel, epilogue),
        grid=(n_col + 1, n_row),
        in_specs=[pl.BlockSpec((BM, k), lambda jj, i: (row(jj, i), 0)),
                  pl.BlockSpec((None, KC_WS, BN),
                               lambda jj, i: (layer, jnp.minimum(i, n_chunk - 1), jnp.minimum(jj, n_col - 1)))]
                 + [pl.BlockSpec((None, 1, BN), lambda jj, i: (layer, 0, col(jj)))] * len(extra),
        out_specs=pl.BlockSpec((BM, BN), lambda jj, i: (row(jj, i), col(jj))),
        out_shape=jax.ShapeDtypeStruct((T_ALL, n), BF16),
        scratch_shapes=[pltpu.VMEM((2, k, BN), BF16)],
        compiler_params=_params(("arbitrary", "arbitrary"),
                                _vmem_limit([BM * k * 2, KC_WS * BN * 4, BM * BN * 2],
                                            scratch_bytes=2 * k * BN * 2, temp_bytes=2 * BM * BN * 4)),
        name=name,
    )(a, w, *extra)


def _in_proj(h, w_in, layer):
    return _ws_matmul(h, w_in, layer, lambda acc: acc, name="in_proj")


def _kv_proj_kernel(h_ref, w_ref, o_ref):
    o_ref[...] = jnp.dot(h_ref[...], w_ref[...].astype(BF16), preferred_element_type=F32)


def _kv_proj(h, w_in, layer):
    return pl.pallas_call(
        _kv_proj_kernel,
        grid=(2 * KV_DIM // BN_KV, T_CTX // BM),
        in_specs=[pl.BlockSpec((BM, D_MODEL), lambda j, i: (i, 0)),
                  pl.BlockSpec((None, D_MODEL, BN_KV), lambda j, i: (layer, 0, OFF_K // BN_KV + j))],
        out_specs=pl.BlockSpec((BM, BN_KV), lambda j, i: (i, j)),
        out_shape=jax.ShapeDtypeStruct((T_CTX, 2 * KV_DIM), F32),
        compiler_params=_params(("arbitrary", "arbitrary"),
                                _vmem_limit([BM * D_MODEL * 2, D_MODEL * BN_KV * 4, BM * BN_KV * 4],
                                            temp_bytes=D_MODEL * BN_KV * 2 + BM * BN_KV * 4)),
        name="kv_proj",
    )(h, w_in)


def _conv_kernel(cb_ref, cc_ref, cx_ref, w_ref, b_ref, o_ref):
    i = pl.program_id(0)
    seq = jnp.where(i < T_CTX // BM, SEQ, DEC_SEQ)
    u = cc_ref[...].astype(F32) * cx_ref[...].astype(F32)
    pos = lax.broadcasted_iota(jnp.int32, u.shape, 0) & (seq - 1)
    prev = jnp.where(pos == 0, 0.0, pltpu.roll(u, 1, 0))
    nxt = jnp.where(pos == seq - 1, 0.0, pltpu.roll(u, BM - 1, 0))
    w = w_ref[...]
    conv = w[0:1] * prev + w[1:2] * u + w[2:3] * nxt + b_ref[...]
    o_ref[...] = (cb_ref[...].astype(F32) * conv).astype(o_ref.dtype)


def _conv_branch(z, conv_w, conv_b, layer):
    nb = CONV_DIM // BN_CONV
    zspec = lambda off: pl.BlockSpec((BM, BN_CONV), lambda i, j: (i, off // BN_CONV + j))
    return pl.pallas_call(
        _conv_kernel,
        grid=(T_ALL // BM, nb),
        in_specs=[zspec(OFF_CB), zspec(OFF_CC), zspec(OFF_CX),
                  pl.BlockSpec((None, 3, BN_CONV), lambda i, j: (layer, 0, j)),
                  pl.BlockSpec((None, 1, BN_CONV), lambda i, j: (layer, 0, j))],
        out_specs=pl.BlockSpec((BM, BN_CONV), lambda i, j: (i, j)),
        out_shape=jax.ShapeDtypeStruct((T_ALL, CONV_DIM), BF16),
        compiler_params=_params(("arbitrary", "arbitrary"),
                                _vmem_limit([BM * BN_CONV * 2] * 4, temp_bytes=6 * BM * BN_CONV * 4)),
        name="conv_branch",
    )(z, z, z, conv_w, conv_b.reshape(DEPTH, 1, CONV_DIM))


def _sink_column(sink_ref, layer, kvh, rows_per_head):
    row = lax.broadcasted_iota(jnp.int32, (GROUP * rows_per_head, 1), 0)
    col = jnp.full(row.shape, sink_ref[layer, kvh * GROUP], F32)
    for g in range(1, GROUP):
        col = jnp.where(row >= g * rows_per_head, sink_ref[layer, kvh * GROUP + g], col)
    return col


def _stack_heads(q):
    return jnp.concatenate([q[:, g * HEAD_DIM:(g + 1) * HEAD_DIM] for g in range(GROUP)], axis=0)


def _qk(q, k):
    return lax.dot_general(q, k, (((1,), (1,)), ((), ())), preferred_element_type=F32)


def _ctx_attn_kernel(layer, sink_ref, q_ref, k_ref, v_ref, o_ref):
    kvh = pl.program_id(1)
    q4 = _stack_heads(q_ref[...])
    s = _qk(q4, k_ref[...]) * HEAD_DIM ** -0.5
    sink = _sink_column(sink_ref, layer, kvh, SEQ)
    m = jnp.maximum(jnp.max(s, axis=-1, keepdims=True), sink)
    e = jnp.exp(s - m)
    p = e * (1.0 / (jnp.sum(e, axis=-1, keepdims=True) + jnp.exp(sink - m)))
    o = jnp.dot(p.astype(BF16), v_ref[...], preferred_element_type=F32)
    for g in range(GROUP):
        o_ref[:, g * HEAD_DIM:(g + 1) * HEAD_DIM] = o[g * SEQ:(g + 1) * SEQ].astype(o_ref.dtype)


def _ctx_attention(z, attn_sink, layer):
    qw = GROUP * HEAD_DIM
    return pl.pallas_call(
        functools.partial(_ctx_attn_kernel, layer),
        grid=(BATCH, N_KV_HEADS),
        in_specs=[pl.BlockSpec(memory_space=pltpu.SMEM),
                  pl.BlockSpec((SEQ, qw), lambda b, h: (b, OFF_Q // qw + h)),
                  pl.BlockSpec((SEQ, HEAD_DIM), lambda b, h: (b, OFF_K // HEAD_DIM + h)),
                  pl.BlockSpec((SEQ, HEAD_DIM), lambda b, h: (b, OFF_V // HEAD_DIM + h))],
        out_specs=pl.BlockSpec((SEQ, qw), lambda b, h: (b, h)),
        out_shape=jax.ShapeDtypeStruct((T_ALL, ATTN_DIM), BF16),
        compiler_params=_params(("arbitrary", "arbitrary"),
                                _vmem_limit([SEQ * qw * 2] * 2, temp_bytes=6 * GROUP * SEQ * SEQ * 4)),
        name="ctx_attention",
    )(attn_sink, z, z, z)


def _rope_tables():
    pos = np.arange(DEC_SEQ)
    row_id = jnp.asarray(pos // GRID_W, F32)
    col_id = jnp.asarray(pos % GRID_W, F32)
    half = HEAD_DIM // 2
    freqs = ROPE_BASE ** (-jnp.arange(0, half, 2, dtype=F32) / half)
    ang_r = row_id[:, None] * freqs[None, :]
    ang_c = col_id[:, None] * freqs[None, :]
    cos = jnp.concatenate([jnp.cos(ang_r), jnp.cos(ang_r), jnp.cos(ang_c), jnp.cos(ang_c)], axis=-1)
    sin = jnp.concatenate([-jnp.sin(ang_r), jnp.sin(ang_r), -jnp.sin(ang_c), jnp.sin(ang_c)], axis=-1)
    return cos, sin


def _rope_kernel(x_ref, cos_ref, sin_ref, o_ref):
    cos = cos_ref[...]
    sin = sin_ref[...]
    quarter = HEAD_DIM // 4
    lane = lax.broadcasted_iota(jnp.int32, cos.shape, 1)
    first = (lane % (2 * quarter)) < quarter
    for g in range(GROUP):
        x = x_ref[:, g * HEAD_DIM:(g + 1) * HEAD_DIM].astype(F32)
        partner = jnp.where(first, pltpu.roll(x, HEAD_DIM - quarter, 1), pltpu.roll(x, quarter, 1))
        o_ref[:, g * HEAD_DIM:(g + 1) * HEAD_DIM] = (x * cos + partner * sin).astype(o_ref.dtype)


def _rope_qk(z, cos, sin):
    w = GROUP * HEAD_DIM
    nb = (ATTN_DIM + KV_DIM) // w
    tab = pl.BlockSpec((DEC_SEQ, HEAD_DIM), lambda b, j: (0, 0))
    return pl.pallas_call(
        _rope_kernel,
        grid=(DEC_BATCH, nb),
        in_specs=[pl.BlockSpec((DEC_SEQ, w), lambda b, j: (T_CTX // DEC_SEQ + b, OFF_Q // w + j)), tab, tab],
        out_specs=pl.BlockSpec((DEC_SEQ, w), lambda b, j: (b, j)),
        out_shape=jax.ShapeDtypeStruct((T_LAT, ATTN_DIM + KV_DIM), BF16),
        compiler_params=_params(("arbitrary", "arbitrary"),
                                _vmem_limit([DEC_SEQ * w * 2] * 2 + [DEC_SEQ * HEAD_DIM * 4] * 2,
                                            temp_bytes=6 * DEC_SEQ * HEAD_DIM * 4)),
        name="rope_qk",
    )(z, cos, sin)


def _lat_attn_kernel(layer, sink_ref, q_ref, k_ref, v_ref, kc_ref, vc_ref, _, o_ref):
    kvh = pl.program_id(1)
    nblk = DEC_SEQ // BLOCK
    scale = HEAD_DIM ** -0.5
    sink = _sink_column(sink_ref, layer, kvh, BLOCK)
    k_ctx = kc_ref[...].astype(BF16)
    v_ctx = vc_ref[...].astype(BF16)
    band_bias = {}

    def bias(width, q_off):
        if (width, q_off) not in band_bias:
            shape = (GROUP * BLOCK, width)
            qpos = q_off + (lax.broadcasted_iota(jnp.int32, shape, 0) & (BLOCK - 1))
            kpos = lax.broadcasted_iota(jnp.int32, shape, 1)
            band_bias[width, q_off] = jnp.where(jnp.abs(kpos - qpos) <= WINDOW, 0.0, -jnp.inf).astype(F32)
        return band_bias[width, q_off]

    for n in range(nblk):
        lo = max(n - 1, 0) * BLOCK
        hi = min(n + 2, nblk) * BLOCK
        q4 = _stack_heads(q_ref[n * BLOCK:(n + 1) * BLOCK, :])
        s_loc = _qk(q4, k_ref[lo:hi, :]) * scale + bias(hi - lo, n * BLOCK - lo)
        s_ctx = _qk(q4, k_ctx) * scale
        m = jnp.maximum(jnp.maximum(jnp.max(s_loc, axis=-1, keepdims=True),
                                    jnp.max(s_ctx, axis=-1, keepdims=True)), sink)
        e_loc = jnp.exp(s_loc - m)
        e_ctx = jnp.exp(s_ctx - m)
        inv = 1.0 / (jnp.sum(e_loc, axis=-1, keepdims=True) + jnp.sum(e_ctx, axis=-1, keepdims=True)
                     + jnp.exp(sink - m))
        o = (jnp.dot((e_loc * inv).astype(BF16), v_ref[lo:hi, :], preferred_element_type=F32)
             + jnp.dot((e_ctx * inv).astype(BF16), v_ctx, preferred_element_type=F32))
        for g in range(GROUP):
            o_ref[n * BLOCK:(n + 1) * BLOCK, g * HEAD_DIM:(g + 1) * HEAD_DIM] = (
                o[g * BLOCK:(g + 1) * BLOCK].astype(o_ref.dtype))


def _lat_attention(qk_rot, z, cache_k, cache_v, attn_sink, layer, y_ctx):
    qw = GROUP * HEAD_DIM
    ck = cache_k.reshape(DEC_BATCH, DEPTH, PAST_LEN, KV_DIM)
    cv = cache_v.reshape(DEC_BATCH, DEPTH, PAST_LEN, KV_DIM)
    cspec = pl.BlockSpec((None, None, PAST_LEN, HEAD_DIM), lambda b, h: (b, layer, 0, h))
    return pl.pallas_call(
        functools.partial(_lat_attn_kernel, layer),
        grid=(DEC_BATCH, N_KV_HEADS),
        in_specs=[pl.BlockSpec(memory_space=pltpu.SMEM),
                  pl.BlockSpec((DEC_SEQ, qw), lambda b, h: (b, h)),
                  pl.BlockSpec((DEC_SEQ, HEAD_DIM), lambda b, h: (b, ATTN_DIM // HEAD_DIM + h)),
                  pl.BlockSpec((DEC_SEQ, HEAD_DIM), lambda b, h: (T_CTX // DEC_SEQ + b, OFF_V // HEAD_DIM + h)),
                  cspec, cspec, pl.BlockSpec(memory_space=pl.ANY)],
        out_specs=pl.BlockSpec((DEC_SEQ, qw), lambda b, h: (T_CTX // DEC_SEQ + b, h)),
        out_shape=jax.ShapeDtypeStruct((T_ALL, ATTN_DIM), BF16),
        input_output_aliases={6: 0},
        compiler_params=_params(("arbitrary", "arbitrary"),
                                _vmem_limit([DEC_SEQ * qw * 2] * 2 + [PAST_LEN * HEAD_DIM * 4] * 2,
                                            temp_bytes=16 << 20)),
        name="lat_attention",
    )(attn_sink, qk_rot, qk_rot, z, ck, cv, y_ctx)


def _log_sigmoid(x):
    return jnp.minimum(x, 0.0) - jnp.log(1.0 + jnp.exp(-jnp.abs(x)))


def _head_norm(o, g):
    mu = jnp.mean(o, axis=-1, keepdims=True)
    d = o - mu
    var = jnp.mean(d * d, axis=-1, keepdims=True)
    return d * lax.rsqrt(var + EPS) * g


def _silu(x):
    return x * jax.nn.sigmoid(x)


def _kt_v(k, v):
    return lax.dot_general(k, v, (((0,), (0,)), ((), ())), preferred_element_type=F32)


def _retention_kernel(seq_len, heads, has_state, n_alias, *refs):
    df_ref, db_ref, rq_ref, rk_ref, rv_ref, gf_ref, gb_ref, gn_ref = refs[:8]
    rest = refs[8:]
    if has_state:
        s0f_ref, s0b_ref = rest[:2]
        rest = rest[2:]
    rest = rest[n_alias:]
    if has_state:
        y_ref, of_scr, ob_scr = rest
    else:
        y_ref, sf_ref, sb_ref, of_scr, ob_scr = rest
    n = seq_len // RET_CHUNK
    c = RET_CHUNK
    ii = lax.broadcasted_iota(jnp.int32, (c, c), 0)
    jj = lax.broadcasted_iota(jnp.int32, (c, c), 1)
    row = lax.broadcasted_iota(jnp.int32, (c, RET_DV), 0).astype(F32)

    def direction(hd, lg, rel, q_pow, k_pow, s, order, o_scr):
        cols = slice(hd * RET_DV, (hd + 1) * RET_DV)
        keep = rel >= 0
        dmat = jnp.where(keep, jnp.exp(jnp.where(keep, rel, 0).astype(F32) * lg[:, :c]), 0.0)
        q_dec = jnp.exp(q_pow * lg)
        k_dec = jnp.exp(k_pow * lg)
        chunk_dec = jnp.exp(c * lg)
        for t in order:
            rows = slice(t * c, (t + 1) * c)
            qc = rq_ref[rows, cols]
            ks = rk_ref[rows, cols].astype(F32) * RET_DK ** -0.5
            vc = rv_ref[rows, cols]
            inner = _qk(qc, ks.astype(BF16)) * dmat
            o = jnp.dot(inner.astype(BF16), vc, preferred_element_type=F32)
            kv = _kt_v((ks * k_dec).astype(BF16), vc)
            if s is None:
                s = kv
            else:
                o = o + jnp.dot(qc, s.astype(BF16), preferred_element_type=F32) * q_dec
                s = chunk_dec * s + kv
            o_scr[rows, cols] = o
        return s

    for hd in range(heads):
        lg_f = _log_sigmoid(df_ref[hd])
        lg_b = _log_sigmoid(db_ref[hd])
        s_f = direction(hd, lg_f, ii - jj, row + 1.0, c - 1.0 - row,
                        s0f_ref[hd] if has_state else None, range(n), of_scr)
        s_b = direction(hd, lg_b, jj - ii, c - row, row,
                        s0b_ref[hd] if has_state else None, range(n - 1, -1, -1), ob_scr)
        if not has_state:
            sf_ref[hd] = s_f
            sb_ref[hd] = s_b
        cols = slice(hd * RET_DV, (hd + 1) * RET_DV)
        gn = gn_ref[hd]
        for t in range(n):
            rows = slice(t * c, (t + 1) * c)
            y = (_silu(gf_ref[rows, cols].astype(F32)) * _head_norm(of_scr[rows, cols], gn)
                 + _silu(gb_ref[rows, cols].astype(F32)) * _head_norm(ob_scr[rows, cols], gn))
            y_ref[rows, cols] = y.astype(y_ref.dtype)


def _retention(z, dec_f, dec_b, gn_g, layer, seq_len, row0, n_seq, heads, in_place=(), states=None):
    has_state = states is not None
    blk0 = row0 // seq_len
    w = heads * RET_DV
    zspec = lambda off: pl.BlockSpec((seq_len, w), lambda b, h: (blk0 + b, off // w + h))
    pspec = pl.BlockSpec((None, heads, 1, RET_DV), lambda b, h: (layer, h, 0, 0))
    sspec = pl.BlockSpec((None, None, heads, RET_DK, RET_DV), lambda b, h: (b, layer, h, 0, 0))
    in_specs = [pspec, pspec, zspec(OFF_RQ), zspec(OFF_RK), zspec(OFF_RV), zspec(OFF_RGF), zspec(OFF_RGB), pspec]
    args = [dec_f, dec_b, z, z, z, z, z, gn_g.reshape(DEPTH, RET_HEADS, 1, RET_DV)]
    if has_state:
        in_specs += [sspec, sspec]
        args += list(states)
    aliases = {}
    for k, prev in enumerate(in_place):
        if prev is not None:
            aliases[len(args)] = k
            in_specs.append(pl.BlockSpec(memory_space=pl.ANY))
            args.append(prev)
    y_spec = pl.BlockSpec((seq_len, w), lambda b, h: (blk0 + b, h))
    y_shape = jax.ShapeDtypeStruct((T_ALL, RET_V_DIM), BF16)
    st_bytes = heads * RET_DK * RET_DV * 4
    if has_state:
        out_specs, out_shape = y_spec, y_shape
    else:
        oshape = jax.ShapeDtypeStruct((n_seq, DEPTH, RET_HEADS, RET_DK, RET_DV), F32)
        out_specs, out_shape = [y_spec, sspec, sspec], [y_shape, oshape, oshape]
    return pl.pallas_call(
        functools.partial(_retention_kernel, seq_len, heads, has_state, len(aliases)),
        grid=(n_seq, RET_HEADS // heads),
        in_specs=in_specs,
        out_specs=out_specs,
        out_shape=out_shape,
        input_output_aliases=aliases,
        scratch_shapes=[pltpu.VMEM((seq_len, w), F32), pltpu.VMEM((seq_len, w), F32)],
        compiler_params=_params(("arbitrary", "arbitrary"),
                                _vmem_limit([seq_len * w * 2] * 6 + [st_bytes] * 2,
                                            scratch_bytes=2 * seq_len * w * 4, temp_bytes=8 << 20)),
        name="retention",
    )(*args)


def _merge_kernel(yc_ref, ya_ref, yr_ref, wc_ref, wa_ref, wr_ref, ga_ref, gb_ref, gr_ref, o_ref):
    def term(y_ref, w_ref, gate_ref):
        proj = jnp.dot(y_ref[...], w_ref[...], preferred_element_type=F32)
        return jax.nn.sigmoid(gate_ref[...].astype(F32)) * proj

    merged = term(yc_ref, wc_ref, ga_ref) + term(ya_ref, wa_ref, gb_ref) + term(yr_ref, wr_ref, gr_ref)
    o_ref[...] = merged.astype(o_ref.dtype)


def _merge(y_conv, y_attn, y_ret, w_conv_out, w_attn_out, w_ret_out, z, layer):
    bn = BN_MERGE
    yspec = pl.BlockSpec((BM, CONV_DIM), lambda i, j: (i, 0))
    wspec = pl.BlockSpec((None, CONV_DIM, bn), lambda i, j: (layer, 0, j))
    gspec = lambda off: pl.BlockSpec((BM, bn), lambda i, j: (i, off // bn + j))
    return pl.pallas_call(
        _merge_kernel,
        grid=(T_ALL // BM, D_MODEL // bn),
        in_specs=[yspec, yspec, yspec, wspec, wspec, wspec, gspec(OFF_GA), gspec(OFF_GB), gspec(OFF_GR)],
        out_specs=pl.BlockSpec((BM, bn), lambda i, j: (i, j)),
        out_shape=jax.ShapeDtypeStruct((T_ALL, D_MODEL), BF16),
        compiler_params=_params(("arbitrary", "arbitrary"),
                                _vmem_limit([BM * CONV_DIM * 2] * 3 + [CONV_DIM * bn * 2] * 3 + [BM * bn * 2] * 4,
                                            temp_bytes=3 * BM * bn * 4)),
        name="merge",
    )(y_conv, y_attn, y_ret, w_conv_out, w_attn_out, w_ret_out, z, z, z)


def _wo_kernel(m_ref, w_ref, x_ref, g_ref, o_ref):
    acc = jnp.dot(m_ref[...], w_ref[...], preferred_element_type=F32)
    o_ref[...] = x_ref[...] + g_ref[...] * acc


def _out_proj(merged, w_o, x, mod4, layer):
    return pl.pallas_call(
        _wo_kernel,
        grid=(T_ALL // BM, D_MODEL // BN),
        in_specs=[pl.BlockSpec((BM, D_MODEL), lambda i, j: (i, 0)),
                  pl.BlockSpec((None, D_MODEL, BN), lambda i, j: (layer, 0, j)),
                  pl.BlockSpec((BM, BN), lambda i, j: (i, j)),
                  _mod_spec(layer, 2, BM, BN, with_j=True)],
        out_specs=pl.BlockSpec((BM, BN), lambda i, j: (i, j)),
        out_shape=jax.ShapeDtypeStruct((T_ALL, D_MODEL), F32),
        compiler_params=_params(("arbitrary", "arbitrary"),
                                _vmem_limit([BM * D_MODEL * 2, D_MODEL * BN * 2, BM * BN * 4, BM * BN * 4],
                                            temp_bytes=2 * BM * BN * 4)),
        name="out_proj",
    )(merged, w_o, x, mod4)


def _relu2_epilogue(acc, b_ref):
    r = jnp.maximum(acc + b_ref[...], 0.0)
    return r * r


def _ff1(h, w_ff1, b_ff1, layer):
    return _ws_matmul(h, w_ff1, layer, _relu2_epilogue, extra=(b_ff1.reshape(DEPTH, 1, D_FF),), name="ff1")


def _ff2_kernel(f_ref, w_ref, b_ref, x_ref, g_ref, o_ref, acc_ref):
    k = pl.program_id(2)

    @pl.when(k == 0)
    def _():
        acc_ref[...] = jnp.zeros_like(acc_ref)

    acc_ref[...] += jnp.dot(f_ref[...], w_ref[...], preferred_element_type=F32)

    @pl.when(k == pl.num_programs(2) - 1)
    def _():
        o_ref[...] = x_ref[...] + g_ref[...] * (acc_ref[...] + b_ref[...])


def _ff2(f, w_ff2, b_ff2, x, mod4, layer):
    bk = BK_FF2
    per = D_MODEL // BN
    return pl.pallas_call(
        _ff2_kernel,
        grid=(T_ALL // BM, D_MODEL // BN, D_FF // bk),
        in_specs=[pl.BlockSpec((BM, bk), lambda i, j, k: (i, k)),
                  pl.BlockSpec((None, bk, BN), lambda i, j, k: (layer, k, j)),
                  pl.BlockSpec((None, 1, BN), lambda i, j, k: (layer, 0, j)),
                  pl.BlockSpec((BM, BN), lambda i, j, k: (i, j)),
                  pl.BlockSpec((None, None, 1, BN), lambda i, j, k: (layer, _cond_row(i, BM), 0, 5 * per + j))],
        out_specs=pl.BlockSpec((BM, BN), lambda i, j, k: (i, j)),
        out_shape=jax.ShapeDtypeStruct((T_ALL, D_MODEL), F32),
        scratch_shapes=[pltpu.VMEM((BM, BN), F32)],
        compiler_params=_params(("arbitrary", "arbitrary", "arbitrary"),
                                _vmem_limit([BM * bk * 2, bk * BN * 2, BM * BN * 4, BM * BN * 4],
                                            scratch_bytes=BM * BN * 4, temp_bytes=2 * BM * BN * 4)),
        name="ff2",
    )(f, w_ff2, b_ff2.reshape(DEPTH, 1, D_MODEL), x, mod4)


def _ret_param(p):
    return jnp.broadcast_to(p.astype(F32)[:, :, None, None], (DEPTH, RET_HEADS, 1, RET_DV))


def kernel(x_prompt, x_sample, c, cache_k, cache_v, state_ret_f, state_ret_b, c_ctx, w_mod, b_mod, norm1_g, w_in, conv_w, conv_b, attn_sink, ret_decay_f, ret_decay_b, ret_gn_g, w_conv_out, w_attn_out, w_ret_out, w_o, norm2_g, w_ff1, b_ff1, w_ff2, b_ff2, final_g):
    x = jnp.concatenate([x_prompt.reshape(T_CTX, D_MODEL), x_sample.reshape(T_LAT, D_MODEL)], axis=0)
    cond = jnp.concatenate([c_ctx[None, :], c, jnp.zeros((COND_PAD - N_COND, D_MODEL), F32)], axis=0)
    mod4 = _adaln(cond, w_mod, b_mod).reshape(DEPTH, COND_PAD, 1, N_MOD * D_MODEL)

    w_conv_b = w_conv_out.astype(BF16)
    w_attn_b = w_attn_out.astype(BF16)
    w_ret_b = w_ret_out.astype(BF16)
    w_o_b = w_o.astype(BF16)
    w_ff2_b = w_ff2.astype(BF16)
    dec_f = _ret_param(ret_decay_f)
    dec_b = _ret_param(ret_decay_b)
    cos, sin = _rope_tables()

    ks, vs = [], []
    new_states = ()
    for l in range(DEPTH):
        h = _norm_mod(x, norm1_g, mod4, l, 0)
        z = _in_proj(h, w_in, l)
        kv = _kv_proj(h, w_in, l)
        ks.append(kv[:, :KV_DIM].reshape(BATCH, SEQ, N_KV_HEADS, HEAD_DIM))
        vs.append(kv[:, KV_DIM:].reshape(BATCH, SEQ, N_KV_HEADS, HEAD_DIM))

        y_conv = _conv_branch(z, conv_w, conv_b, l)
        y_attn = _lat_attention(_rope_qk(z, cos, sin), z, cache_k, cache_v, attn_sink, l,
                                _ctx_attention(z, attn_sink, l))
        y_ret, *new_states = _retention(z, dec_f, dec_b, ret_gn_g, l, SEQ, 0, BATCH, RET_HEADS_PER_STEP_CTX,
                                        in_place=(None,) + tuple(new_states) if new_states else ())
        y_ret = _retention(z, dec_f, dec_b, ret_gn_g, l, DEC_SEQ, T_CTX, DEC_BATCH, 1,
                           in_place=(y_ret,), states=(state_ret_f, state_ret_b))

        merged = _merge(y_conv, y_attn, y_ret, w_conv_b, w_attn_b, w_ret_b, z, l)
        x = _out_proj(merged, w_o_b, x, mod4, l)
        h = _norm_mod(x, norm2_g, mod4, l, 3)
        f = _ff1(h, w_ff1, b_ff1, l)
        x = _ff2(f, w_ff2_b, b_ff2, x, mod4, l)

    y_prompt = _final_norm(x, final_g, 0, T_CTX).reshape(BATCH, SEQ, D_MODEL)
    y_sample = _final_norm(x, final_g, T_CTX, T_LAT).reshape(DEC_BATCH, DEC_SEQ, D_MODEL)
    return (y_prompt, y_sample, jnp.stack(ks, axis=1), jnp.stack(vs, axis=1), new_states[0], new_states[1])
```

```python
import functools

import numpy as np
import jax
import jax.numpy as jnp
from jax import lax
from jax.experimental import pallas as pl
from jax.experimental.pallas import tpu as pltpu

F32 = jnp.float32
BF16 = jnp.bfloat16

D_MODEL = 4096
BATCH = 16
SEQ = 256
DEPTH = 2
DEC_BATCH = 8
DEC_SEQ = 1024
PAST_LEN = 256
GRID_W = 64
EPS = 1e-6
N_MOD = 6
CONV_DIM = 2048
N_Q_HEADS = 16
N_KV_HEADS = 4
GROUP = N_Q_HEADS // N_KV_HEADS
HEAD_DIM = 128
ATTN_DIM = N_Q_HEADS * HEAD_DIM
KV_DIM = N_KV_HEADS * HEAD_DIM
WINDOW = 128
BLOCK = 128
ROPE_BASE = 10000.0
RET_HEADS = 8
RET_DK = 256
RET_DV = 256
RET_V_DIM = RET_HEADS * RET_DV
RET_CHUNK = 128
D_FF = 4 * D_MODEL

T_CTX = BATCH * SEQ
T_LAT = DEC_BATCH * DEC_SEQ
T_ALL = T_CTX + T_LAT
N_COND = 1 + DEC_BATCH
COND_PAD = 16

OFF_CB = 0
OFF_CC = OFF_CB + CONV_DIM
OFF_CX = OFF_CC + CONV_DIM
OFF_Q = OFF_CX + CONV_DIM
OFF_K = OFF_Q + ATTN_DIM
OFF_V = OFF_K + KV_DIM
OFF_RQ = OFF_V + KV_DIM
OFF_RK = OFF_RQ + RET_HEADS * RET_DK
OFF_RV = OFF_RK + RET_HEADS * RET_DK
OFF_RGF = OFF_RV + RET_V_DIM
OFF_RGB = OFF_RGF + RET_V_DIM
OFF_GA = OFF_RGB + RET_V_DIM
OFF_GB = OFF_GA + D_MODEL
OFF_GR = OFF_GB + D_MODEL
IN_COLS = OFF_GR + D_MODEL

V7X_VMEM_LIMIT_CAP = 60000 * 1024
BM = 1024
BN = 1024
BK_FF2 = 2048
BM_NORM = 256
NORM_ROWS = 16
BN_MERGE = 512
BN_ADALN = 512
BN_CONV = 512
RET_HEADS_PER_STEP_CTX = 4
RET_HEADS_PER_STEP_LAT = 2
KC_WS = 512
SIDE_ROWS = 64
BN_KV = 512
assert OFF_K % BN_KV == 0 and OFF_V == OFF_K + KV_DIM
assert BM == DEC_SEQ and T_CTX % BM == 0 and BM % SEQ == 0
assert RET_DK == 4 ** 4


def _vmem_limit(block_bytes, scratch_bytes=0, temp_bytes=0):
    need = 2 * sum(block_bytes) + scratch_bytes + temp_bytes + (2 << 20)
    return int(min(need, V7X_VMEM_LIMIT_CAP))


def _params(semantics, vmem):
    return pltpu.CompilerParams(dimension_semantics=semantics, vmem_limit_bytes=vmem)


def _cond_row(i, bm):
    n_ctx = T_CTX // bm
    per_seq = DEC_SEQ // bm
    return jnp.where(i < n_ctx, 0, 1 + (i - n_ctx) // per_seq)


def _adaln_kernel(c_ref, w_ref, b_ref, o_ref):
    s = c_ref[...]
    s = s * jax.nn.sigmoid(s)
    acc = jnp.dot(s.astype(BF16), w_ref[...].astype(BF16), preferred_element_type=F32)
    o_ref[...] = acc + b_ref[...]


def _adaln(cond, w_mod, b_mod):
    n = N_MOD * D_MODEL
    return pl.pallas_call(
        _adaln_kernel,
        grid=(DEPTH, n // BN_ADALN),
        in_specs=[pl.BlockSpec((COND_PAD, D_MODEL), lambda l, j: (0, 0)),
                  pl.BlockSpec((None, D_MODEL, BN_ADALN), lambda l, j: (l, 0, j)),
                  pl.BlockSpec((None, 1, BN_ADALN), lambda l, j: (l, 0, j))],
        out_specs=pl.BlockSpec((None, COND_PAD, BN_ADALN), lambda l, j: (l, 0, j)),
        out_shape=jax.ShapeDtypeStruct((DEPTH, COND_PAD, n), F32),
        compiler_params=_params(("arbitrary", "arbitrary"),
                                _vmem_limit([D_MODEL * BN_ADALN * 4], temp_bytes=D_MODEL * BN_ADALN * 2)),
        name="adaln",
    )(cond, w_mod, b_mod.reshape(DEPTH, 1, n))


def _rms(x, g):
    y = x * lax.rsqrt(jnp.mean(x * x, axis=-1, keepdims=True) + EPS)
    return y * g


def _norm_mod_kernel(x_ref, g_ref, sh_ref, sc_ref, *rest):
    o_ref = rest[-1]
    g = g_ref[...]
    scale1 = 1 + sc_ref[...]
    shift = sh_ref[...]
    for r in range(0, BM_NORM, NORM_ROWS):
        rows = slice(r, r + NORM_ROWS)
        o_ref[rows, :] = (_rms(x_ref[rows, :], g) * scale1 + shift).astype(o_ref.dtype)


def _norm_mod(x, g, mod4, layer, which_shift, row0=0, in_place=None):
    blk0 = row0 // BM_NORM
    mspec = lambda which: pl.BlockSpec((None, None, 1, D_MODEL),
                                       lambda i: (layer, _cond_row(blk0 + i, BM_NORM), 0, which))
    extra = [] if in_place is None else [in_place]
    return pl.pallas_call(
        _norm_mod_kernel,
        grid=(x.shape[0] // BM_NORM,),
        in_specs=[pl.BlockSpec((BM_NORM, D_MODEL), lambda i: (i, 0)),
                  pl.BlockSpec((None, 1, D_MODEL), lambda i: (layer, 0, 0)),
                  mspec(which_shift), mspec(which_shift + 1)]
                 + [pl.BlockSpec(memory_space=pl.ANY)] * len(extra),
        out_specs=pl.BlockSpec((BM_NORM, D_MODEL), lambda i: (blk0 + i, 0)),
        out_shape=jax.ShapeDtypeStruct((T_ALL, D_MODEL), BF16),
        input_output_aliases={4: 0} if extra else {},
        compiler_params=_params(("arbitrary",), _vmem_limit([BM_NORM * D_MODEL * 4, BM_NORM * D_MODEL * 2],
                                                            temp_bytes=2 * BM_NORM * D_MODEL * 4)),
        name="norm_mod",
    )(x, g.reshape(DEPTH, 1, D_MODEL), mod4, mod4, *extra)


def _norm_kernel(x_ref, g_ref, o_ref):
    g = g_ref[...]
    for r in range(0, BM_NORM, NORM_ROWS):
        rows = slice(r, r + NORM_ROWS)
        o_ref[rows, :] = _rms(x_ref[rows, :], g)


def _final_norm(x, g, row0, rows):
    blk0 = row0 // BM_NORM
    return pl.pallas_call(
        _norm_kernel,
        grid=(rows // BM_NORM,),
        in_specs=[pl.BlockSpec((BM_NORM, D_MODEL), lambda i: (blk0 + i, 0)),
                  pl.BlockSpec((1, D_MODEL), lambda i: (0, 0))],
        out_specs=pl.BlockSpec((BM_NORM, D_MODEL), lambda i: (i, 0)),
        out_shape=jax.ShapeDtypeStruct((rows, D_MODEL), F32),
        compiler_params=_params(("arbitrary",), _vmem_limit([BM_NORM * D_MODEL * 4] * 2,
                                                            temp_bytes=2 * BM_NORM * D_MODEL * 4)),
        name="final_norm",
    )(x, g.reshape(1, D_MODEL))


def _ws_matmul_kernel(epilogue, side_blocks, a_ref, wc_ref, *rest):
    n_side = len(side_blocks)
    n_extra = len(rest) - 2 * n_side - 2
    extra = rest[:n_extra]
    side_in = rest[n_extra:n_extra + n_side]
    o_ref = rest[n_extra + n_side]
    side_out = rest[n_extra + n_side + 1:-1]
    wbf_scr = rest[-1]
    jj = pl.program_id(0)
    i = pl.program_id(1)
    n_col = pl.num_programs(0) - 1
    n_chunk = wbf_scr.shape[1] // KC_WS

    step = jj * pl.num_programs(1) + i
    for (first, count), src, dst in zip(side_blocks, side_in, side_out):
        @pl.when(jnp.logical_and(step >= first, step < first + count))
        def _(src=src, dst=dst):
            dst[...] = src[...].astype(BF16)

    @pl.when(jnp.logical_and(jj < n_col, i < n_chunk))
    def _():
        r0 = pl.multiple_of(i * KC_WS, KC_WS)
        wbf_scr[jj % 2, pl.ds(r0, KC_WS), :] = wc_ref[...].astype(BF16)

    @pl.when(jj > 0)
    def _():
        acc = jnp.dot(a_ref[...], wbf_scr[(jj - 1) % 2], preferred_element_type=F32)
        o_ref[...] = epilogue(acc, *extra).astype(o_ref.dtype)


def _ws_matmul(a, w, layer, epilogue, extra=(), side=(), name="ws_matmul"):
    k, n = w.shape[1:]
    n_col = n // BN
    n_row = T_ALL // BM
    n_chunk = k // KC_WS
    assert n_row >= n_chunk and k % KC_WS == 0 and n % BN == 0
    row = lambda jj, i: jnp.where(jj == 0, 0, i)
    col = lambda jj: jnp.maximum(jj - 1, 0)

    side_blocks, side_in, side_out, side_shape = [], [], [], []
    first = 0
    for s in side:
        count = s.shape[1] // SIDE_ROWS
        assert s.shape[1] % SIDE_ROWS == 0 and s.shape[2] == D_MODEL
        blk = lambda jj, i, first=first, count=count: jnp.clip(jj * n_row + i - first, 0, count - 1)
        side_blocks.append((first, count))
        side_in.append(pl.BlockSpec((None, SIDE_ROWS, D_MODEL), lambda jj, i, blk=blk: (layer, blk(jj, i), 0)))
        side_out.append(pl.BlockSpec((SIDE_ROWS, D_MODEL), lambda jj, i, blk=blk: (blk(jj, i), 0)))
        side_shape.append(jax.ShapeDtypeStruct((s.shape[1], D_MODEL), BF16))
        first += count
    assert first <= (n_col + 1) * n_row

    out = pl.pallas_call(
        functools.partial(_ws_matmul_kernel, epilogue, tuple(side_blocks)),
        grid=(n_col + 1, n_row),
        in_specs=[pl.BlockSpec((BM, k), lambda jj, i: (row(jj, i), 0)),
                  pl.BlockSpec((None, KC_WS, BN),
                               lambda jj, i: (layer, jnp.minimum(i, n_chunk - 1), jnp.minimum(jj, n_col - 1)))]
                 + [pl.BlockSpec((None, 1, BN), lambda jj, i: (layer, 0, col(jj)))] * len(extra) + side_in,
        out_specs=[pl.BlockSpec((BM, BN), lambda jj, i: (row(jj, i), col(jj)))] + side_out,
        out_shape=[jax.ShapeDtypeStruct((T_ALL, n), BF16)] + side_shape,
        scratch_shapes=[pltpu.VMEM((2, k, BN), BF16)],
        compiler_params=_params(("arbitrary", "arbitrary"),
                                _vmem_limit([BM * k * 2, KC_WS * BN * 4, BM * BN * 2]
                                            + [SIDE_ROWS * D_MODEL * 6] * len(side),
                                            scratch_bytes=2 * k * BN * 2, temp_bytes=2 * BM * BN * 4)),
        name=name,
    )(a, w, *extra, *side)
    return out if side else out[0]


def _in_proj(h, w_in, layer, side):
    return _ws_matmul(h, w_in, layer, lambda acc: acc, side=side, name="in_proj")


def _kv_proj_kernel(h_ref, w_ref, o_ref):
    o_ref[...] = jnp.dot(h_ref[...], w_ref[...].astype(BF16), preferred_element_type=F32)


def _kv_proj(h, w_in, layer):
    return pl.pallas_call(
        _kv_proj_kernel,
        grid=(2 * KV_DIM // BN_KV, T_CTX // BM),
        in_specs=[pl.BlockSpec((BM, D_MODEL), lambda j, i: (i, 0)),
                  pl.BlockSpec((None, D_MODEL, BN_KV), lambda j, i: (layer, 0, OFF_K // BN_KV + j))],
        out_specs=pl.BlockSpec((BM, BN_KV), lambda j, i: (i, j)),
        out_shape=jax.ShapeDtypeStruct((T_CTX, 2 * KV_DIM), F32),
        compiler_params=_params(("arbitrary", "arbitrary"),
                                _vmem_limit([BM * D_MODEL * 2, D_MODEL * BN_KV * 4, BM * BN_KV * 4],
                                            temp_bytes=D_MODEL * BN_KV * 2 + BM * BN_KV * 4)),
        name="kv_proj",
    )(h, w_in)


def _conv_kernel(cb_ref, cc_ref, cx_ref, w_ref, b_ref, o_ref):
    i = pl.program_id(0)
    seq = jnp.where(i < T_CTX // BM, SEQ, DEC_SEQ)
    u = cc_ref[...].astype(F32) * cx_ref[...].astype(F32)
    pos = lax.broadcasted_iota(jnp.int32, u.shape, 0) & (seq - 1)
    prev = jnp.where(pos == 0, 0.0, pltpu.roll(u, 1, 0))
    nxt = jnp.where(pos == seq - 1, 0.0, pltpu.roll(u, BM - 1, 0))
    w = w_ref[...]
    conv = w[0:1] * prev + w[1:2] * u + w[2:3] * nxt + b_ref[...]
    o_ref[...] = (cb_ref[...].astype(F32) * conv).astype(o_ref.dtype)


def _conv_branch(z, conv_w, conv_b, layer):
    nb = CONV_DIM // BN_CONV
    zspec = lambda off: pl.BlockSpec((BM, BN_CONV), lambda i, j: (i, off // BN_CONV + j))
    return pl.pallas_call(
        _conv_kernel,
        grid=(T_ALL // BM, nb),
        in_specs=[zspec(OFF_CB), zspec(OFF_CC), zspec(OFF_CX),
                  pl.BlockSpec((None, 3, BN_CONV), lambda i, j: (layer, 0, j)),
                  pl.BlockSpec((None, 1, BN_CONV), lambda i, j: (layer, 0, j))],
        out_specs=pl.BlockSpec((BM, BN_CONV), lambda i, j: (i, j)),
        out_shape=jax.ShapeDtypeStruct((T_ALL, CONV_DIM), BF16),
        compiler_params=_params(("arbitrary", "arbitrary"),
                                _vmem_limit([BM * BN_CONV * 2] * 4, temp_bytes=6 * BM * BN_CONV * 4)),
        name="conv_branch",
    )(z, z, z, conv_w, conv_b.reshape(DEPTH, 1, CONV_DIM))


def _sink_column(sink_ref, layer, kvh, rows_per_head):
    row = lax.broadcasted_iota(jnp.int32, (GROUP * rows_per_head, 1), 0)
    col = jnp.full(row.shape, sink_ref[layer, kvh * GROUP], F32)
    for g in range(1, GROUP):
        col = jnp.where(row >= g * rows_per_head, sink_ref[layer, kvh * GROUP + g], col)
    return col


def _stack_heads(q):
    return jnp.concatenate([q[:, g * HEAD_DIM:(g + 1) * HEAD_DIM] for g in range(GROUP)], axis=0)


def _qk(q, k):
    return lax.dot_general(q, k, (((1,), (1,)), ((), ())), preferred_element_type=F32)


def _ctx_attn_kernel(layer, sink_ref, q_ref, k_ref, v_ref, o_ref):
    qw = GROUP * HEAD_DIM
    for kvh in range(N_KV_HEADS):
        q4 = _stack_heads(q_ref[:, kvh * qw:(kvh + 1) * qw])
        s = _qk(q4, k_ref[:, kvh * HEAD_DIM:(kvh + 1) * HEAD_DIM]) * HEAD_DIM ** -0.5
        sink = _sink_column(sink_ref, layer, kvh, SEQ)
        m = jnp.maximum(jnp.max(s, axis=-1, keepdims=True), sink)
        e = jnp.exp(s - m)
        p = e * (1.0 / (jnp.sum(e, axis=-1, keepdims=True) + jnp.exp(sink - m)))
        o = jnp.dot(p.astype(BF16), v_ref[:, kvh * HEAD_DIM:(kvh + 1) * HEAD_DIM], preferred_element_type=F32)
        for g in range(GROUP):
            c0 = (kvh * GROUP + g) * HEAD_DIM
            o_ref[:, c0:c0 + HEAD_DIM] = o[g * SEQ:(g + 1) * SEQ].astype(o_ref.dtype)


def _ctx_attention(z, attn_sink, layer):
    return pl.pallas_call(
        functools.partial(_ctx_attn_kernel, layer),
        grid=(BATCH,),
        in_specs=[pl.BlockSpec(memory_space=pltpu.SMEM),
                  pl.BlockSpec((SEQ, ATTN_DIM), lambda b: (b, OFF_Q // ATTN_DIM)),
                  pl.BlockSpec((SEQ, KV_DIM), lambda b: (b, OFF_K // KV_DIM)),
                  pl.BlockSpec((SEQ, KV_DIM), lambda b: (b, OFF_V // KV_DIM))],
        out_specs=pl.BlockSpec((SEQ, ATTN_DIM), lambda b: (b, 0)),
        out_shape=jax.ShapeDtypeStruct((T_ALL, ATTN_DIM), BF16),
        compiler_params=_params(("arbitrary",),
                                _vmem_limit([SEQ * ATTN_DIM * 2] * 2 + [SEQ * KV_DIM * 2] * 2,
                                            temp_bytes=6 * N_Q_HEADS * SEQ * SEQ * 4)),
        name="ctx_attention",
    )(attn_sink, z, z, z)


def _rope_tables():
    pos = np.arange(DEC_SEQ)
    row_id = jnp.asarray(pos // GRID_W, F32)
    col_id = jnp.asarray(pos % GRID_W, F32)
    half = HEAD_DIM // 2
    freqs = ROPE_BASE ** (-jnp.arange(0, half, 2, dtype=F32) / half)
    ang_r = row_id[:, None] * freqs[None, :]
    ang_c = col_id[:, None] * freqs[None, :]
    cos = jnp.concatenate([jnp.cos(ang_r), jnp.cos(ang_r), jnp.cos(ang_c), jnp.cos(ang_c)], axis=-1)
    sin = jnp.concatenate([-jnp.sin(ang_r), jnp.sin(ang_r), -jnp.sin(ang_c), jnp.sin(ang_c)], axis=-1)
    return cos, sin


def _rope_kernel(x_ref, cos_ref, sin_ref, o_ref):
    cos = cos_ref[...]
    sin = sin_ref[...]
    quarter = HEAD_DIM // 4
    lane = lax.broadcasted_iota(jnp.int32, cos.shape, 1)
    first = (lane % (2 * quarter)) < quarter
    for g in range(GROUP):
        x = x_ref[:, g * HEAD_DIM:(g + 1) * HEAD_DIM].astype(F32)
        partner = jnp.where(first, pltpu.roll(x, HEAD_DIM - quarter, 1), pltpu.roll(x, quarter, 1))
        o_ref[:, g * HEAD_DIM:(g + 1) * HEAD_DIM] = (x * cos + partner * sin).astype(o_ref.dtype)


def _rope_qk(z, cos, sin):
    w = GROUP * HEAD_DIM
    nb = (ATTN_DIM + KV_DIM) // w
    tab = pl.BlockSpec((DEC_SEQ, HEAD_DIM), lambda b, j: (0, 0))
    return pl.pallas_call(
        _rope_kernel,
        grid=(DEC_BATCH, nb),
        in_specs=[pl.BlockSpec((DEC_SEQ, w), lambda b, j: (T_CTX // DEC_SEQ + b, OFF_Q // w + j)), tab, tab],
        out_specs=pl.BlockSpec((DEC_SEQ, w), lambda b, j: (b, j)),
        out_shape=jax.ShapeDtypeStruct((T_LAT, ATTN_DIM + KV_DIM), BF16),
        compiler_params=_params(("arbitrary", "arbitrary"),
                                _vmem_limit([DEC_SEQ * w * 2] * 2 + [DEC_SEQ * HEAD_DIM * 4] * 2,
                                            temp_bytes=6 * DEC_SEQ * HEAD_DIM * 4)),
        name="rope_qk",
    )(z, cos, sin)


def _lat_attn_kernel(layer, sink_ref, q_ref, k_ref, v_ref, kc_ref, vc_ref, _, o_ref):
    kvh = pl.program_id(1)
    nblk = DEC_SEQ // BLOCK
    scale = HEAD_DIM ** -0.5
    sink = _sink_column(sink_ref, layer, kvh, BLOCK)
    k_ctx = kc_ref[...].astype(BF16)
    v_ctx = vc_ref[...].astype(BF16)
    band_bias = {}

    def bias(width, q_off):
        if (width, q_off) not in band_bias:
            shape = (GROUP * BLOCK, width)
            qpos = q_off + (lax.broadcasted_iota(jnp.int32, shape, 0) & (BLOCK - 1))
            kpos = lax.broadcasted_iota(jnp.int32, shape, 1)
            band_bias[width, q_off] = jnp.where(jnp.abs(kpos - qpos) <= WINDOW, 0.0, -jnp.inf).astype(F32)
        return band_bias[width, q_off]

    for n in range(nblk):
        lo = max(n - 1, 0) * BLOCK
        hi = min(n + 2, nblk) * BLOCK
        q4 = _stack_heads(q_ref[n * BLOCK:(n + 1) * BLOCK, :])
        s_loc = _qk(q4, k_ref[lo:hi, :]) * scale + bias(hi - lo, n * BLOCK - lo)
        s_ctx = _qk(q4, k_ctx) * scale
        m = jnp.maximum(jnp.maximum(jnp.max(s_loc, axis=-1, keepdims=True),
                                    jnp.max(s_ctx, axis=-1, keepdims=True)), sink)
        e_loc = jnp.exp(s_loc - m)
        e_ctx = jnp.exp(s_ctx - m)
        inv = 1.0 / (jnp.sum(e_loc, axis=-1, keepdims=True) + jnp.sum(e_ctx, axis=-1, keepdims=True)
                     + jnp.exp(sink - m))
        o = (jnp.dot((e_loc * inv).astype(BF16), v_ref[lo:hi, :], preferred_element_type=F32)
             + jnp.dot((e_ctx * inv).astype(BF16), v_ctx, preferred_element_type=F32))
        for g in range(GROUP):
            o_ref[n * BLOCK:(n + 1) * BLOCK, g * HEAD_DIM:(g + 1) * HEAD_DIM] = (
                o[g * BLOCK:(g + 1) * BLOCK].astype(o_ref.dtype))


def _lat_attention(qk_rot, z, cache_k, cache_v, attn_sink, layer, y_ctx):
    qw = GROUP * HEAD_DIM
    ck = cache_k.reshape(DEC_BATCH, DEPTH, PAST_LEN, KV_DIM)
    cv = cache_v.reshape(DEC_BATCH, DEPTH, PAST_LEN, KV_DIM)
    cspec = pl.BlockSpec((None, None, PAST_LEN, HEAD_DIM), lambda b, h: (b, layer, 0, h))
    return pl.pallas_call(
        functools.partial(_lat_attn_kernel, layer),
        grid=(DEC_BATCH, N_KV_HEADS),
        in_specs=[pl.BlockSpec(memory_space=pltpu.SMEM),
                  pl.BlockSpec((DEC_SEQ, qw), lambda b, h: (b, h)),
                  pl.BlockSpec((DEC_SEQ, HEAD_DIM), lambda b, h: (b, ATTN_DIM // HEAD_DIM + h)),
                  pl.BlockSpec((DEC_SEQ, HEAD_DIM), lambda b, h: (T_CTX // DEC_SEQ + b, OFF_V // HEAD_DIM + h)),
                  cspec, cspec, pl.BlockSpec(memory_space=pl.ANY)],
        out_specs=pl.BlockSpec((DEC_SEQ, qw), lambda b, h: (T_CTX // DEC_SEQ + b, h)),
        out_shape=jax.ShapeDtypeStruct((T_ALL, ATTN_DIM), BF16),
        input_output_aliases={6: 0},
        compiler_params=_params(("arbitrary", "arbitrary"),
                                _vmem_limit([DEC_SEQ * qw * 2] * 2 + [PAST_LEN * HEAD_DIM * 4] * 2,
                                            temp_bytes=16 << 20)),
        name="lat_attention",
    )(attn_sink, qk_rot, qk_rot, z, ck, cv, y_ctx)


def _log_sigmoid(x):
    return jnp.minimum(x, 0.0) - jnp.log(1.0 + jnp.exp(-jnp.abs(x)))


def _head_norm(o, g):
    mu = jnp.mean(o, axis=-1, keepdims=True)
    d = o - mu
    var = jnp.mean(d * d, axis=-1, keepdims=True)
    return d * lax.rsqrt(var + EPS) * g


def _silu(x):
    return x * jax.nn.sigmoid(x)


def _kt_v(k, v):
    return lax.dot_general(k, v, (((0,), (0,)), ((), ())), preferred_element_type=F32)


def _retention_kernel(seq_len, heads, has_state, n_alias, *refs):
    df_ref, db_ref, rq_ref, rk_ref, rv_ref, gf_ref, gb_ref, gn_ref = refs[:8]
    rest = refs[8:]
    if has_state:
        s0f_ref, s0b_ref = rest[:2]
        rest = rest[2:]
    rest = rest[n_alias:]
    if has_state:
        y_ref, of_scr, ob_scr = rest
    else:
        y_ref, sf_ref, sb_ref, of_scr, ob_scr = rest
    n = seq_len // RET_CHUNK
    c = RET_CHUNK
    ii = lax.broadcasted_iota(jnp.int32, (c, c), 0)
    jj = lax.broadcasted_iota(jnp.int32, (c, c), 1)
    row = lax.broadcasted_iota(jnp.int32, (c, RET_DV), 0).astype(F32)

    def direction(hd, lg, rel, q_pow, k_pow, s, order, o_scr):
        cols = slice(hd * RET_DV, (hd + 1) * RET_DV)
        keep = rel >= 0
        k_scale = RET_DK ** -0.5
        dmat = jnp.where(keep, jnp.exp(jnp.where(keep, rel, 0).astype(F32) * lg[:, :c]), 0.0) * k_scale
        q_dec = jnp.exp(q_pow * lg)
        k_dec = jnp.exp(k_pow * lg) * k_scale
        chunk_dec = jnp.exp(c * lg)
        for t in order:
            rows = slice(t * c, (t + 1) * c)
            qc = rq_ref[rows, cols]
            kc = rk_ref[rows, cols]
            vc = rv_ref[rows, cols]
            inner = _qk(qc, kc) * dmat
            o = jnp.dot(inner.astype(BF16), vc, preferred_element_type=F32)
            kv = _kt_v((kc.astype(F32) * k_dec).astype(BF16), vc)
            if s is None:
                s = kv
            else:
                o = o + jnp.dot(qc, s.astype(BF16), preferred_element_type=F32) * q_dec
                s = chunk_dec * s + kv
            o_scr[rows, cols] = o
        return s

    for hd in range(heads):
        lg_f = _log_sigmoid(df_ref[hd])
        lg_b = _log_sigmoid(db_ref[hd])
        s_f = direction(hd, lg_f, ii - jj, row + 1.0, c - 1.0 - row,
                        s0f_ref[hd] if has_state else None, range(n), of_scr)
        s_b = direction(hd, lg_b, jj - ii, c - row, row,
                        s0b_ref[hd] if has_state else None, range(n - 1, -1, -1), ob_scr)
        if not has_state:
            sf_ref[hd] = s_f
            sb_ref[hd] = s_b
        cols = slice(hd * RET_DV, (hd + 1) * RET_DV)
        gn = gn_ref[hd]
        for t in range(n):
            rows = slice(t * c, (t + 1) * c)
            y = (_silu(gf_ref[rows, cols].astype(F32)) * _head_norm(of_scr[rows, cols], gn)
                 + _silu(gb_ref[rows, cols].astype(F32)) * _head_norm(ob_scr[rows, cols], gn))
            y_ref[rows, cols] = y.astype(y_ref.dtype)


def _retention(z, dec_f, dec_b, gn_g, layer, seq_len, row0, n_seq, heads, in_place=(), states=None):
    has_state = states is not None
    blk0 = row0 // seq_len
    w = heads * RET_DV
    zspec = lambda off: pl.BlockSpec((seq_len, w), lambda b, h: (blk0 + b, off // w + h))
    pspec = pl.BlockSpec((None, heads, 1, RET_DV), lambda b, h: (layer, h, 0, 0))
    sspec = pl.BlockSpec((None, None, heads, RET_DK, RET_DV), lambda b, h: (b, layer, h, 0, 0))
    in_specs = [pspec, pspec, zspec(OFF_RQ), zspec(OFF_RK), zspec(OFF_RV), zspec(OFF_RGF), zspec(OFF_RGB), pspec]
    args = [dec_f, dec_b, z, z, z, z, z, gn_g.reshape(DEPTH, RET_HEADS, 1, RET_DV)]
    if has_state:
        in_specs += [sspec, sspec]
        args += list(states)
    aliases = {}
    for k, prev in enumerate(in_place):
        if prev is not None:
            aliases[len(args)] = k
            in_specs.append(pl.BlockSpec(memory_space=pl.ANY))
            args.append(prev)
    y_spec = pl.BlockSpec((seq_len, w), lambda b, h: (blk0 + b, h))
    y_shape = jax.ShapeDtypeStruct((T_ALL, RET_V_DIM), BF16)
    st_bytes = heads * RET_DK * RET_DV * 4
    if has_state:
        out_specs, out_shape = y_spec, y_shape
    else:
        oshape = jax.ShapeDtypeStruct((n_seq, DEPTH, RET_HEADS, RET_DK, RET_DV), F32)
        out_specs, out_shape = [y_spec, sspec, sspec], [y_shape, oshape, oshape]
    return pl.pallas_call(
        functools.partial(_retention_kernel, seq_len, heads, has_state, len(aliases)),
        grid=(n_seq, RET_HEADS // heads),
        in_specs=in_specs,
        out_specs=out_specs,
        out_shape=out_shape,
        input_output_aliases=aliases,
        scratch_shapes=[pltpu.VMEM((seq_len, w), F32), pltpu.VMEM((seq_len, w), F32)],
        compiler_params=_params(("arbitrary", "arbitrary"),
                                _vmem_limit([seq_len * w * 2] * 6 + [st_bytes] * 2,
                                            scratch_bytes=2 * seq_len * w * 4, temp_bytes=8 << 20)),
        name="retention",
    )(*args)


def _merge_kernel(yc_ref, ya_ref, yr_ref, wc_ref, wa_ref, wr_ref, ga_ref, gb_ref, gr_ref, o_ref):
    def term(y_ref, w_ref, gate_ref):
        proj = jnp.dot(y_ref[...], w_ref[...], preferred_element_type=F32)
        return jax.nn.sigmoid(gate_ref[...].astype(F32)) * proj

    merged = term(yc_ref, wc_ref, ga_ref) + term(ya_ref, wa_ref, gb_ref) + term(yr_ref, wr_ref, gr_ref)
    o_ref[...] = merged.astype(o_ref.dtype)


def _merge(y_conv, y_attn, y_ret, w_conv_out, w_attn_out, w_ret_out, z, layer):
    bn = BN_MERGE
    yspec = pl.BlockSpec((BM, CONV_DIM), lambda i, j: (i, 0))
    wspec = pl.BlockSpec((None, CONV_DIM, bn), lambda i, j: (layer, 0, j))
    gspec = lambda off: pl.BlockSpec((BM, bn), lambda i, j: (i, off // bn + j))
    return pl.pallas_call(
        _merge_kernel,
        grid=(T_ALL // BM, D_MODEL // bn),
        in_specs=[yspec, yspec, yspec, wspec, wspec, wspec, gspec(OFF_GA), gspec(OFF_GB), gspec(OFF_GR)],
        out_specs=pl.BlockSpec((BM, bn), lambda i, j: (i, j)),
        out_shape=jax.ShapeDtypeStruct((T_ALL, D_MODEL), BF16),
        compiler_params=_params(("arbitrary", "arbitrary"),
                                _vmem_limit([BM * CONV_DIM * 2] * 3 + [CONV_DIM * bn * 2] * 3 + [BM * bn * 2] * 4,
                                            temp_bytes=3 * BM * bn * 4)),
        name="merge",
    )(y_conv, y_attn, y_ret, w_conv_out, w_attn_out, w_ret_out, z, z, z)


def _wo_kernel(m_ref, w_ref, x_ref, g_ref, *rest):
    o_ref = rest[-1]
    acc = jnp.dot(m_ref[...], w_ref[...], preferred_element_type=F32)
    o_ref[...] = x_ref[...] + g_ref[...] * acc


def _out_proj(merged, w_o, x, mod4, layer, row0=0, in_place=None):
    blk0 = row0 // BM
    per = D_MODEL // BN
    extra = [] if in_place is None else [in_place]
    return pl.pallas_call(
        _wo_kernel,
        grid=(x.shape[0] // BM, D_MODEL // BN),
        in_specs=[pl.BlockSpec((BM, D_MODEL), lambda i, j: (blk0 + i, 0)),
                  pl.BlockSpec((D_MODEL, BN), lambda i, j: (0, j)),
                  pl.BlockSpec((BM, BN), lambda i, j: (i, j)),
                  pl.BlockSpec((None, None, 1, BN),
                               lambda i, j: (layer, _cond_row(blk0 + i, BM), 0, 2 * per + j))]
                 + [pl.BlockSpec(memory_space=pl.ANY)] * len(extra),
        out_specs=pl.BlockSpec((BM, BN), lambda i, j: (blk0 + i, j)),
        out_shape=jax.ShapeDtypeStruct((T_ALL, D_MODEL), F32),
        input_output_aliases={4: 0} if extra else {},
        compiler_params=_params(("arbitrary", "arbitrary"),
                                _vmem_limit([BM * D_MODEL * 2, D_MODEL * BN * 2, BM * BN * 4, BM * BN * 4],
                                            temp_bytes=2 * BM * BN * 4)),
        name="out_proj",
    )(merged, w_o, x, mod4, *extra)


def _relu2_epilogue(acc, b_ref):
    r = jnp.maximum(acc + b_ref[...], 0.0)
    return r * r


def _ff1(h, w_ff1, b_ff1, layer):
    return _ws_matmul(h, w_ff1, layer, _relu2_epilogue, extra=(b_ff1.reshape(DEPTH, 1, D_FF),), name="ff1")


def _ff2_kernel(f_ref, w_ref, b_ref, x_ref, g_ref, o_ref, acc_ref):
    k = pl.program_id(2)

    @pl.when(k == 0)
    def _():
        acc_ref[...] = jnp.zeros_like(acc_ref)

    acc_ref[...] += jnp.dot(f_ref[...], w_ref[...], preferred_element_type=F32)

    @pl.when(k == pl.num_programs(2) - 1)
    def _():
        o_ref[...] = x_ref[...] + g_ref[...] * (acc_ref[...] + b_ref[...])


def _ff2(f, w_ff2, b_ff2, x, mod4, layer):
    bk = BK_FF2
    per = D_MODEL // BN
    return pl.pallas_call(
        _ff2_kernel,
        grid=(T_ALL // BM, D_MODEL // BN, D_FF // bk),
        in_specs=[pl.BlockSpec((BM, bk), lambda i, j, k: (i, k)),
                  pl.BlockSpec((bk, BN), lambda i, j, k: (k, j)),
                  pl.BlockSpec((None, 1, BN), lambda i, j, k: (layer, 0, j)),
                  pl.BlockSpec((BM, BN), lambda i, j, k: (i, j)),
                  pl.BlockSpec((None, None, 1, BN), lambda i, j, k: (layer, _cond_row(i, BM), 0, 5 * per + j))],
        out_specs=pl.BlockSpec((BM, BN), lambda i, j, k: (i, j)),
        out_shape=jax.ShapeDtypeStruct((T_ALL, D_MODEL), F32),
        scratch_shapes=[pltpu.VMEM((BM, BN), F32)],
        compiler_params=_params(("arbitrary", "arbitrary", "arbitrary"),
                                _vmem_limit([BM * bk * 2, bk * BN * 2, BM * BN * 4, BM * BN * 4],
                                            scratch_bytes=BM * BN * 4, temp_bytes=2 * BM * BN * 4)),
        name="ff2",
    )(f, w_ff2, b_ff2.reshape(DEPTH, 1, D_MODEL), x, mod4)


def _ret_param(p):
    return jnp.broadcast_to(p.astype(F32)[:, :, None, None], (DEPTH, RET_HEADS, 1, RET_DV))


def kernel(x_prompt, x_sample, c, cache_k, cache_v, state_ret_f, state_ret_b, c_ctx, w_mod, b_mod, norm1_g, w_in, conv_w, conv_b, attn_sink, ret_decay_f, ret_decay_b, ret_gn_g, w_conv_out, w_attn_out, w_ret_out, w_o, norm2_g, w_ff1, b_ff1, w_ff2, b_ff2, final_g):
    x_parts = ((x_prompt.reshape(T_CTX, D_MODEL), 0), (x_sample.reshape(T_LAT, D_MODEL), T_CTX))
    cond = jnp.concatenate([c_ctx[None, :], c, jnp.zeros((COND_PAD - N_COND, D_MODEL), F32)], axis=0)
    mod4 = _adaln(cond, w_mod, b_mod).reshape(DEPTH, COND_PAD, 1, N_MOD * D_MODEL)

    w_conv_b = w_conv_out.astype(BF16)
    w_attn_b = w_attn_out.astype(BF16)
    w_ret_b = w_ret_out.astype(BF16)
    dec_f = _ret_param(ret_decay_f)
    dec_b = _ret_param(ret_decay_b)
    cos, sin = _rope_tables()

    ks, vs = [], []
    new_states = ()
    for l in range(DEPTH):
        h = None
        for x_rows, row0 in x_parts:
            h = _norm_mod(x_rows, norm1_g, mod4, l, 0, row0, in_place=h)
        z, w_ff2_b, w_o_b = _in_proj(h, w_in, l, side=(w_ff2, w_o))
        kv = _kv_proj(h, w_in, l)
        ks.append(kv[:, :KV_DIM].reshape(BATCH, SEQ, N_KV_HEADS, HEAD_DIM))
        vs.append(kv[:, KV_DIM:].reshape(BATCH, SEQ, N_KV_HEADS, HEAD_DIM))

        y_conv = _conv_branch(z, conv_w, conv_b, l)
        y_attn = _lat_attention(_rope_qk(z, cos, sin), z, cache_k, cache_v, attn_sink, l,
                                _ctx_attention(z, attn_sink, l))
        y_ret, *new_states = _retention(z, dec_f, dec_b, ret_gn_g, l, SEQ, 0, BATCH, RET_HEADS_PER_STEP_CTX,
                                        in_place=(None,) + tuple(new_states) if new_states else ())
        y_ret = _retention(z, dec_f, dec_b, ret_gn_g, l, DEC_SEQ, T_CTX, DEC_BATCH, RET_HEADS_PER_STEP_LAT,
                           in_place=(y_ret,), states=(state_ret_f, state_ret_b))

        merged = _merge(y_conv, y_attn, y_ret, w_conv_b, w_attn_b, w_ret_b, z, l)
        x = None
        for x_rows, row0 in x_parts:
            x = _out_proj(merged, w_o_b, x_rows, mod4, l, row0, in_place=x)
        h = _norm_mod(x, norm2_g, mod4, l, 3)
        f = _ff1(h, w_ff1, b_ff1, l)
        x = _ff2(f, w_ff2_b, b_ff2, x, mod4, l)
        x_parts = ((x, 0),)

    y_prompt = _final_norm(x, final_g, 0, T_CTX).reshape(BATCH, SEQ, D_MODEL)
    y_sample = _final_norm(x, final_g, T_CTX, T_LAT).reshape(DEC_BATCH, DEC_SEQ, D_MODEL)
    return (y_prompt, y_sample, jnp.stack(ks, axis=1), jnp.stack(vs, axis=1), new_states[0], new_states[1])
```

```python
import functools

import numpy as np
import jax
import jax.numpy as jnp
from jax import lax
from jax.experimental import pallas as pl
from jax.experimental.pallas import tpu as pltpu

F32 = jnp.float32
BF16 = jnp.bfloat16
LOG2_E = float(np.log2(np.e))

D_MODEL = 4096
BATCH = 16
SEQ = 256
DEPTH = 2
DEC_BATCH = 8
DEC_SEQ = 1024
PAST_LEN = 256
GRID_W = 64
EPS = 1e-6
N_MOD = 6
CONV_DIM = 2048
N_Q_HEADS = 16
N_KV_HEADS = 4
GROUP = N_Q_HEADS // N_KV_HEADS
HEAD_DIM = 128
ATTN_DIM = N_Q_HEADS * HEAD_DIM
KV_DIM = N_KV_HEADS * HEAD_DIM
WINDOW = 128
BLOCK = 128
ROPE_BASE = 10000.0
RET_HEADS = 8
RET_DK = 256
RET_DV = 256
RET_V_DIM = RET_HEADS * RET_DV
RET_CHUNK = 128
D_FF = 4 * D_MODEL

T_CTX = BATCH * SEQ
T_LAT = DEC_BATCH * DEC_SEQ
T_ALL = T_CTX + T_LAT
N_COND = 1 + DEC_BATCH
COND_PAD = 16

OFF_CB = 0
OFF_CC = OFF_CB + CONV_DIM
OFF_CX = OFF_CC + CONV_DIM
OFF_Q = OFF_CX + CONV_DIM
OFF_K = OFF_Q + ATTN_DIM
OFF_V = OFF_K + KV_DIM
OFF_RQ = OFF_V + KV_DIM
OFF_RK = OFF_RQ + RET_HEADS * RET_DK
OFF_RV = OFF_RK + RET_HEADS * RET_DK
OFF_RGF = OFF_RV + RET_V_DIM
OFF_RGB = OFF_RGF + RET_V_DIM
OFF_GA = OFF_RGB + RET_V_DIM
OFF_GB = OFF_GA + D_MODEL
OFF_GR = OFF_GB + D_MODEL
IN_COLS = OFF_GR + D_MODEL

V7X_VMEM_LIMIT_CAP = 60000 * 1024
BM = 1024
BN = 1024
BK_FF2 = 2048
BM_NORM = 256
NORM_ROWS = 16
BN_MERGE = 512
BN_ADALN = 512
BN_CONV = 512
RET_HEADS_PER_STEP_CTX = 4
RET_HEADS_PER_STEP_LAT = 2
KC_WS = 512
SIDE_ROWS = 64
BN_KV = 512
assert OFF_K % BN_KV == 0 and OFF_V == OFF_K + KV_DIM
assert BM == DEC_SEQ and T_CTX % BM == 0 and BM % SEQ == 0
assert RET_DK == 4 ** 4


def _vmem_limit(block_bytes, scratch_bytes=0, temp_bytes=0):
    need = 2 * sum(block_bytes) + scratch_bytes + temp_bytes + (2 << 20)
    return int(min(need, V7X_VMEM_LIMIT_CAP))


def _params(semantics, vmem):
    return pltpu.CompilerParams(dimension_semantics=semantics, vmem_limit_bytes=vmem)


def _cond_row(i, bm):
    n_ctx = T_CTX // bm
    per_seq = DEC_SEQ // bm
    return jnp.where(i < n_ctx, 0, 1 + (i - n_ctx) // per_seq)


def _adaln_kernel(c_ref, w_ref, b_ref, o_ref):
    s = c_ref[...]
    s = s * jax.nn.sigmoid(s)
    acc = jnp.dot(s.astype(BF16), w_ref[...].astype(BF16), preferred_element_type=F32)
    o_ref[...] = acc + b_ref[...]


def _adaln(cond, w_mod, b_mod):
    n = N_MOD * D_MODEL
    return pl.pallas_call(
        _adaln_kernel,
        grid=(DEPTH, n // BN_ADALN),
        in_specs=[pl.BlockSpec((COND_PAD, D_MODEL), lambda l, j: (0, 0)),
                  pl.BlockSpec((None, D_MODEL, BN_ADALN), lambda l, j: (l, 0, j)),
                  pl.BlockSpec((None, 1, BN_ADALN), lambda l, j: (l, 0, j))],
        out_specs=pl.BlockSpec((None, COND_PAD, BN_ADALN), lambda l, j: (l, 0, j)),
        out_shape=jax.ShapeDtypeStruct((DEPTH, COND_PAD, n), F32),
        compiler_params=_params(("arbitrary", "arbitrary"),
                                _vmem_limit([D_MODEL * BN_ADALN * 4], temp_bytes=D_MODEL * BN_ADALN * 2)),
        name="adaln",
    )(cond, w_mod, b_mod.reshape(DEPTH, 1, n))


def _rms(x, g):
    y = x * lax.rsqrt(jnp.mean(x * x, axis=-1, keepdims=True) + EPS)
    return y * g


def _norm_mod_kernel(x_ref, g_ref, sh_ref, sc_ref, *rest):
    o_ref = rest[-1]
    g = g_ref[...]
    scale1 = 1 + sc_ref[...]
    shift = sh_ref[...]
    for r in range(0, BM_NORM, NORM_ROWS):
        rows = slice(r, r + NORM_ROWS)
        o_ref[rows, :] = (_rms(x_ref[rows, :], g) * scale1 + shift).astype(o_ref.dtype)


def _norm_mod(x, g, mod4, layer, which_shift, row0=0, in_place=None):
    blk0 = row0 // BM_NORM
    mspec = lambda which: pl.BlockSpec((None, None, 1, D_MODEL),
                                       lambda i: (layer, _cond_row(blk0 + i, BM_NORM), 0, which))
    extra = [] if in_place is None else [in_place]
    return pl.pallas_call(
        _norm_mod_kernel,
        grid=(x.shape[0] // BM_NORM,),
        in_specs=[pl.BlockSpec((BM_NORM, D_MODEL), lambda i: (i, 0)),
                  pl.BlockSpec((None, 1, D_MODEL), lambda i: (layer, 0, 0)),
                  mspec(which_shift), mspec(which_shift + 1)]
                 + [pl.BlockSpec(memory_space=pl.ANY)] * len(extra),
        out_specs=pl.BlockSpec((BM_NORM, D_MODEL), lambda i: (blk0 + i, 0)),
        out_shape=jax.ShapeDtypeStruct((T_ALL, D_MODEL), BF16),
        input_output_aliases={4: 0} if extra else {},
        compiler_params=_params(("arbitrary",), _vmem_limit([BM_NORM * D_MODEL * 4, BM_NORM * D_MODEL * 2],
                                                            temp_bytes=2 * BM_NORM * D_MODEL * 4)),
        name="norm_mod",
    )(x, g.reshape(DEPTH, 1, D_MODEL), mod4, mod4, *extra)


def _norm_kernel(x_ref, g_ref, o_ref):
    g = g_ref[...]
    for r in range(0, BM_NORM, NORM_ROWS):
        rows = slice(r, r + NORM_ROWS)
        o_ref[rows, :] = _rms(x_ref[rows, :], g)


def _final_norm(x, g, row0, rows):
    blk0 = row0 // BM_NORM
    return pl.pallas_call(
        _norm_kernel,
        grid=(rows // BM_NORM,),
        in_specs=[pl.BlockSpec((BM_NORM, D_MODEL), lambda i: (blk0 + i, 0)),
                  pl.BlockSpec((1, D_MODEL), lambda i: (0, 0))],
        out_specs=pl.BlockSpec((BM_NORM, D_MODEL), lambda i: (i, 0)),
        out_shape=jax.ShapeDtypeStruct((rows, D_MODEL), F32),
        compiler_params=_params(("arbitrary",), _vmem_limit([BM_NORM * D_MODEL * 4] * 2,
                                                            temp_bytes=2 * BM_NORM * D_MODEL * 4)),
        name="final_norm",
    )(x, g.reshape(1, D_MODEL))


def _ws_matmul_kernel(epilogue, side_blocks, a_ref, wc_ref, *rest):
    n_side = len(side_blocks)
    n_extra = len(rest) - 2 * n_side - 2
    extra = rest[:n_extra]
    side_in = rest[n_extra:n_extra + n_side]
    o_ref = rest[n_extra + n_side]
    side_out = rest[n_extra + n_side + 1:-1]
    wbf_scr = rest[-1]
    jj = pl.program_id(0)
    i = pl.program_id(1)
    n_col = pl.num_programs(0) - 1
    n_chunk = wbf_scr.shape[1] // KC_WS

    step = jj * pl.num_programs(1) + i
    for (first, count), src, dst in zip(side_blocks, side_in, side_out):
        @pl.when(jnp.logical_and(step >= first, step < first + count))
        def _(src=src, dst=dst):
            dst[...] = src[...].astype(BF16)

    @pl.when(jnp.logical_and(jj < n_col, i < n_chunk))
    def _():
        r0 = pl.multiple_of(i * KC_WS, KC_WS)
        wbf_scr[jj % 2, pl.ds(r0, KC_WS), :] = wc_ref[...].astype(BF16)

    @pl.when(jj > 0)
    def _():
        acc = jnp.dot(a_ref[...], wbf_scr[(jj - 1) % 2], preferred_element_type=F32)
        o_ref[...] = epilogue(acc, *extra).astype(o_ref.dtype)


def _ws_matmul(a, w, layer, epilogue, extra=(), side=(), name="ws_matmul"):
    k, n = w.shape[1:]
    n_col = n // BN
    n_row = T_ALL // BM
    n_chunk = k // KC_WS
    assert n_row >= n_chunk and k % KC_WS == 0 and n % BN == 0
    row = lambda jj, i: jnp.where(jj == 0, 0, i)
    col = lambda jj: jnp.maximum(jj - 1, 0)

    side_blocks, side_in, side_out, side_shape = [], [], [], []
    first = 0
    for s in side:
        count = s.shape[1] // SIDE_ROWS
        assert s.shape[1] % SIDE_ROWS == 0 and s.shape[2] == D_MODEL
        blk = lambda jj, i, first=first, count=count: jnp.clip(jj * n_row + i - first, 0, count - 1)
        side_blocks.append((first, count))
        side_in.append(pl.BlockSpec((None, SIDE_ROWS, D_MODEL), lambda jj, i, blk=blk: (layer, blk(jj, i), 0)))
        side_out.append(pl.BlockSpec((SIDE_ROWS, D_MODEL), lambda jj, i, blk=blk: (blk(jj, i), 0)))
        side_shape.append(jax.ShapeDtypeStruct((s.shape[1], D_MODEL), BF16))
        first += count
    assert first <= (n_col + 1) * n_row

    out = pl.pallas_call(
        functools.partial(_ws_matmul_kernel, epilogue, tuple(side_blocks)),
        grid=(n_col + 1, n_row),
        in_specs=[pl.BlockSpec((BM, k), lambda jj, i: (row(jj, i), 0)),
                  pl.BlockSpec((None, KC_WS, BN),
                               lambda jj, i: (layer, jnp.minimum(i, n_chunk - 1), jnp.minimum(jj, n_col - 1)))]
                 + [pl.BlockSpec((None, 1, BN), lambda jj, i: (layer, 0, col(jj)))] * len(extra) + side_in,
        out_specs=[pl.BlockSpec((BM, BN), lambda jj, i: (row(jj, i), col(jj)))] + side_out,
        out_shape=[jax.ShapeDtypeStruct((T_ALL, n), BF16)] + side_shape,
        scratch_shapes=[pltpu.VMEM((2, k, BN), BF16)],
        compiler_params=_params(("arbitrary", "arbitrary"),
                                _vmem_limit([BM * k * 2, KC_WS * BN * 4, BM * BN * 2]
                                            + [SIDE_ROWS * D_MODEL * 6] * len(side),
                                            scratch_bytes=2 * k * BN * 2, temp_bytes=2 * BM * BN * 4)),
        name=name,
    )(a, w, *extra, *side)
    return out if side else out[0]


def _in_proj(h, w_in, layer, side):
    return _ws_matmul(h, w_in, layer, lambda acc: acc, side=side, name="in_proj")


def _kv_proj_kernel(h_ref, w_ref, o_ref):
    o_ref[...] = jnp.dot(h_ref[...], w_ref[...].astype(BF16), preferred_element_type=F32)


def _kv_proj(h, w_in, layer):
    return pl.pallas_call(
        _kv_proj_kernel,
        grid=(2 * KV_DIM // BN_KV, T_CTX // BM),
        in_specs=[pl.BlockSpec((BM, D_MODEL), lambda j, i: (i, 0)),
                  pl.BlockSpec((None, D_MODEL, BN_KV), lambda j, i: (layer, 0, OFF_K // BN_KV + j))],
        out_specs=pl.BlockSpec((BM, BN_KV), lambda j, i: (i, j)),
        out_shape=jax.ShapeDtypeStruct((T_CTX, 2 * KV_DIM), F32),
        compiler_params=_params(("arbitrary", "arbitrary"),
                                _vmem_limit([BM * D_MODEL * 2, D_MODEL * BN_KV * 4, BM * BN_KV * 4],
                                            temp_bytes=D_MODEL * BN_KV * 2 + BM * BN_KV * 4)),
        name="kv_proj",
    )(h, w_in)


def _conv_kernel(cb_ref, cc_ref, cx_ref, w_ref, b_ref, o_ref):
    i = pl.program_id(0)
    seq = jnp.where(i < T_CTX // BM, SEQ, DEC_SEQ)
    u = cc_ref[...].astype(F32) * cx_ref[...].astype(F32)
    pos = lax.broadcasted_iota(jnp.int32, u.shape, 0) & (seq - 1)
    prev = jnp.where(pos == 0, 0.0, pltpu.roll(u, 1, 0))
    nxt = jnp.where(pos == seq - 1, 0.0, pltpu.roll(u, BM - 1, 0))
    w = w_ref[...]
    conv = w[0:1] * prev + w[1:2] * u + w[2:3] * nxt + b_ref[...]
    o_ref[...] = (cb_ref[...].astype(F32) * conv).astype(o_ref.dtype)


def _conv_branch(z, conv_w, conv_b, layer):
    nb = CONV_DIM // BN_CONV
    zspec = lambda off: pl.BlockSpec((BM, BN_CONV), lambda i, j: (i, off // BN_CONV + j))
    return pl.pallas_call(
        _conv_kernel,
        grid=(T_ALL // BM, nb),
        in_specs=[zspec(OFF_CB), zspec(OFF_CC), zspec(OFF_CX),
                  pl.BlockSpec((None, 3, BN_CONV), lambda i, j: (layer, 0, j)),
                  pl.BlockSpec((None, 1, BN_CONV), lambda i, j: (layer, 0, j))],
        out_specs=pl.BlockSpec((BM, BN_CONV), lambda i, j: (i, j)),
        out_shape=jax.ShapeDtypeStruct((T_ALL, CONV_DIM), BF16),
        compiler_params=_params(("arbitrary", "arbitrary"),
                                _vmem_limit([BM * BN_CONV * 2] * 4, temp_bytes=6 * BM * BN_CONV * 4)),
        name="conv_branch",
    )(z, z, z, conv_w, conv_b.reshape(DEPTH, 1, CONV_DIM))


def _sink_column(sink_ref, layer, kvh, rows_per_head):
    row = lax.broadcasted_iota(jnp.int32, (GROUP * rows_per_head, 1), 0)
    col = jnp.full(row.shape, sink_ref[layer, kvh * GROUP], F32)
    for g in range(1, GROUP):
        col = jnp.where(row >= g * rows_per_head, sink_ref[layer, kvh * GROUP + g], col)
    return col


def _stack_heads(q):
    return jnp.concatenate([q[:, g * HEAD_DIM:(g + 1) * HEAD_DIM] for g in range(GROUP)], axis=0)


def _qk(q, k):
    return lax.dot_general(q, k, (((1,), (1,)), ((), ())), preferred_element_type=F32)


def _ctx_attn_kernel(layer, sink_ref, q_ref, k_ref, v_ref, o_ref):
    qw = GROUP * HEAD_DIM
    for kvh in range(N_KV_HEADS):
        q4 = _stack_heads(q_ref[:, kvh * qw:(kvh + 1) * qw])
        t = _qk(q4, k_ref[:, kvh * HEAD_DIM:(kvh + 1) * HEAD_DIM]) * (HEAD_DIM ** -0.5 * LOG2_E)
        sink2 = _sink_column(sink_ref, layer, kvh, SEQ) * LOG2_E
        m = jnp.maximum(jnp.max(t, axis=-1, keepdims=True), sink2)
        e = jnp.exp2(t - m)
        inv = 1.0 / (jnp.sum(e, axis=-1, keepdims=True) + jnp.exp2(sink2 - m))
        o = jnp.dot(e.astype(BF16), v_ref[:, kvh * HEAD_DIM:(kvh + 1) * HEAD_DIM],
                    preferred_element_type=F32) * inv
        for g in range(GROUP):
            c0 = (kvh * GROUP + g) * HEAD_DIM
            o_ref[:, c0:c0 + HEAD_DIM] = o[g * SEQ:(g + 1) * SEQ].astype(o_ref.dtype)


def _ctx_attention(z, attn_sink, layer):
    return pl.pallas_call(
        functools.partial(_ctx_attn_kernel, layer),
        grid=(BATCH,),
        in_specs=[pl.BlockSpec(memory_space=pltpu.SMEM),
                  pl.BlockSpec((SEQ, ATTN_DIM), lambda b: (b, OFF_Q // ATTN_DIM)),
                  pl.BlockSpec((SEQ, KV_DIM), lambda b: (b, OFF_K // KV_DIM)),
                  pl.BlockSpec((SEQ, KV_DIM), lambda b: (b, OFF_V // KV_DIM))],
        out_specs=pl.BlockSpec((SEQ, ATTN_DIM), lambda b: (b, 0)),
        out_shape=jax.ShapeDtypeStruct((T_ALL, ATTN_DIM), BF16),
        compiler_params=_params(("arbitrary",),
                                _vmem_limit([SEQ * ATTN_DIM * 2] * 2 + [SEQ * KV_DIM * 2] * 2,
                                            temp_bytes=6 * N_Q_HEADS * SEQ * SEQ * 4)),
        name="ctx_attention",
    )(attn_sink, z, z, z)


def _rope_tables():
    pos = np.arange(DEC_SEQ)
    row_id = jnp.asarray(pos // GRID_W, F32)
    col_id = jnp.asarray(pos % GRID_W, F32)
    half = HEAD_DIM // 2
    freqs = ROPE_BASE ** (-jnp.arange(0, half, 2, dtype=F32) / half)
    ang_r = row_id[:, None] * freqs[None, :]
    ang_c = col_id[:, None] * freqs[None, :]
    cos = jnp.concatenate([jnp.cos(ang_r), jnp.cos(ang_r), jnp.cos(ang_c), jnp.cos(ang_c)], axis=-1)
    sin = jnp.concatenate([-jnp.sin(ang_r), jnp.sin(ang_r), -jnp.sin(ang_c), jnp.sin(ang_c)], axis=-1)
    return cos, sin


def _rope_kernel(x_ref, cos_ref, sin_ref, o_ref):
    cos = cos_ref[...]
    sin = sin_ref[...]
    quarter = HEAD_DIM // 4
    lane = lax.broadcasted_iota(jnp.int32, cos.shape, 1)
    first = (lane % (2 * quarter)) < quarter
    for g in range(GROUP):
        x = x_ref[:, g * HEAD_DIM:(g + 1) * HEAD_DIM].astype(F32)
        partner = jnp.where(first, pltpu.roll(x, HEAD_DIM - quarter, 1), pltpu.roll(x, quarter, 1))
        o_ref[:, g * HEAD_DIM:(g + 1) * HEAD_DIM] = (x * cos + partner * sin).astype(o_ref.dtype)


def _rope_qk(z, cos, sin):
    w = GROUP * HEAD_DIM
    nb = (ATTN_DIM + KV_DIM) // w
    tab = pl.BlockSpec((DEC_SEQ, HEAD_DIM), lambda b, j: (0, 0))
    return pl.pallas_call(
        _rope_kernel,
        grid=(DEC_BATCH, nb),
        in_specs=[pl.BlockSpec((DEC_SEQ, w), lambda b, j: (T_CTX // DEC_SEQ + b, OFF_Q // w + j)), tab, tab],
        out_specs=pl.BlockSpec((DEC_SEQ, w), lambda b, j: (b, j)),
        out_shape=jax.ShapeDtypeStruct((T_LAT, ATTN_DIM + KV_DIM), BF16),
        compiler_params=_params(("arbitrary", "arbitrary"),
                                _vmem_limit([DEC_SEQ * w * 2] * 2 + [DEC_SEQ * HEAD_DIM * 4] * 2,
                                            temp_bytes=6 * DEC_SEQ * HEAD_DIM * 4)),
        name="rope_qk",
    )(z, cos, sin)


def _lat_attn_kernel(layer, sink_ref, q_ref, k_ref, v_ref, kc_ref, vc_ref, _, o_ref):
    kvh = pl.program_id(1)
    nblk = DEC_SEQ // BLOCK
    scale2 = HEAD_DIM ** -0.5 * LOG2_E
    sink2 = _sink_column(sink_ref, layer, kvh, BLOCK) * LOG2_E
    k_ctx = kc_ref[...].astype(BF16)
    v_ctx = vc_ref[...].astype(BF16)
    band_bias = {}

    def bias(width, q_off):
        if (width, q_off) not in band_bias:
            shape = (GROUP * BLOCK, width)
            qpos = q_off + (lax.broadcasted_iota(jnp.int32, shape, 0) & (BLOCK - 1))
            kpos = lax.broadcasted_iota(jnp.int32, shape, 1)
            band_bias[width, q_off] = jnp.where(jnp.abs(kpos - qpos) <= WINDOW, 0.0, -jnp.inf).astype(F32)
        return band_bias[width, q_off]

    for n in range(nblk):
        lo = max(n - 1, 0) * BLOCK
        hi = min(n + 2, nblk) * BLOCK
        q4 = _stack_heads(q_ref[n * BLOCK:(n + 1) * BLOCK, :])
        t_loc = _qk(q4, k_ref[lo:hi, :]) * scale2 + bias(hi - lo, n * BLOCK - lo)
        t_ctx = _qk(q4, k_ctx) * scale2
        m = jnp.maximum(jnp.maximum(jnp.max(t_loc, axis=-1, keepdims=True),
                                    jnp.max(t_ctx, axis=-1, keepdims=True)), sink2)
        e_loc = jnp.exp2(t_loc - m)
        e_ctx = jnp.exp2(t_ctx - m)
        inv = 1.0 / (jnp.sum(e_loc, axis=-1, keepdims=True) + jnp.sum(e_ctx, axis=-1, keepdims=True)
                     + jnp.exp2(sink2 - m))
        o = (jnp.dot(e_loc.astype(BF16), v_ref[lo:hi, :], preferred_element_type=F32)
             + jnp.dot(e_ctx.astype(BF16), v_ctx, preferred_element_type=F32)) * inv
        for g in range(GROUP):
            o_ref[n * BLOCK:(n + 1) * BLOCK, g * HEAD_DIM:(g + 1) * HEAD_DIM] = (
                o[g * BLOCK:(g + 1) * BLOCK].astype(o_ref.dtype))


def _lat_attention(qk_rot, z, cache_k, cache_v, attn_sink, layer, y_ctx):
    qw = GROUP * HEAD_DIM
    ck = cache_k.reshape(DEC_BATCH, DEPTH, PAST_LEN, KV_DIM)
    cv = cache_v.reshape(DEC_BATCH, DEPTH, PAST_LEN, KV_DIM)
    cspec = pl.BlockSpec((None, None, PAST_LEN, HEAD_DIM), lambda b, h: (b, layer, 0, h))
    return pl.pallas_call(
        functools.partial(_lat_attn_kernel, layer),
        grid=(DEC_BATCH, N_KV_HEADS),
        in_specs=[pl.BlockSpec(memory_space=pltpu.SMEM),
                  pl.BlockSpec((DEC_SEQ, qw), lambda b, h: (b, h)),
                  pl.BlockSpec((DEC_SEQ, HEAD_DIM), lambda b, h: (b, ATTN_DIM // HEAD_DIM + h)),
                  pl.BlockSpec((DEC_SEQ, HEAD_DIM), lambda b, h: (T_CTX // DEC_SEQ + b, OFF_V // HEAD_DIM + h)),
                  cspec, cspec, pl.BlockSpec(memory_space=pl.ANY)],
        out_specs=pl.BlockSpec((DEC_SEQ, qw), lambda b, h: (T_CTX // DEC_SEQ + b, h)),
        out_shape=jax.ShapeDtypeStruct((T_ALL, ATTN_DIM), BF16),
        input_output_aliases={6: 0},
        compiler_params=_params(("arbitrary", "arbitrary"),
                                _vmem_limit([DEC_SEQ * qw * 2] * 2 + [PAST_LEN * HEAD_DIM * 4] * 2,
                                            temp_bytes=16 << 20)),
        name="lat_attention",
    )(attn_sink, qk_rot, qk_rot, z, ck, cv, y_ctx)


def _log_sigmoid(x):
    return jnp.minimum(x, 0.0) - jnp.log(1.0 + jnp.exp(-jnp.abs(x)))


def _head_norm(o, g):
    mu = jnp.mean(o, axis=-1, keepdims=True)
    d = o - mu
    var = jnp.mean(d * d, axis=-1, keepdims=True)
    return d * lax.rsqrt(var + EPS) * g


def _silu(x):
    return x * jax.nn.sigmoid(x)


def _kt_v(k, v):
    return lax.dot_general(k, v, (((0,), (0,)), ((), ())), preferred_element_type=F32)


def _retention_kernel(seq_len, heads, has_state, n_alias, *refs):
    df_ref, db_ref, rq_ref, rk_ref, rv_ref, gf_ref, gb_ref, gn_ref = refs[:8]
    rest = refs[8:]
    if has_state:
        s0f_ref, s0b_ref = rest[:2]
        rest = rest[2:]
    rest = rest[n_alias:]
    if has_state:
        y_ref, of_scr, ob_scr = rest
    else:
        y_ref, sf_ref, sb_ref, of_scr, ob_scr = rest
    n = seq_len // RET_CHUNK
    c = RET_CHUNK
    ii = lax.broadcasted_iota(jnp.int32, (c, c), 0)
    jj = lax.broadcasted_iota(jnp.int32, (c, c), 1)
    row = lax.broadcasted_iota(jnp.int32, (c, RET_DV), 0).astype(F32)

    def direction(hd, lg, rel, q_pow, k_pow, s, order, o_scr):
        cols = slice(hd * RET_DV, (hd + 1) * RET_DV)
        keep = rel >= 0
        k_scale = RET_DK ** -0.5
        dmat = jnp.where(keep, jnp.exp(jnp.where(keep, rel, 0).astype(F32) * lg[:, :c]), 0.0) * k_scale
        q_dec = jnp.exp(q_pow * lg)
        k_dec = jnp.exp(k_pow * lg) * k_scale
        chunk_dec = jnp.exp(c * lg)
        for t in order:
            rows = slice(t * c, (t + 1) * c)
            qc = rq_ref[rows, cols]
            kc = rk_ref[rows, cols]
            vc = rv_ref[rows, cols]
            inner = _qk(qc, kc) * dmat
            o = jnp.dot(inner.astype(BF16), vc, preferred_element_type=F32)
            kv = _kt_v((kc.astype(F32) * k_dec).astype(BF16), vc)
            if s is None:
                s = kv
            else:
                o = o + jnp.dot(qc, s.astype(BF16), preferred_element_type=F32) * q_dec
                s = chunk_dec * s + kv
            o_scr[rows, cols] = o
        return s

    for hd in range(heads):
        lg_f = _log_sigmoid(df_ref[hd])
        lg_b = _log_sigmoid(db_ref[hd])
        s_f = direction(hd, lg_f, ii - jj, row + 1.0, c - 1.0 - row,
                        s0f_ref[hd] if has_state else None, range(n), of_scr)
        s_b = direction(hd, lg_b, jj - ii, c - row, row,
                        s0b_ref[hd] if has_state else None, range(n - 1, -1, -1), ob_scr)
        if not has_state:
            sf_ref[hd] = s_f
            sb_ref[hd] = s_b
        cols = slice(hd * RET_DV, (hd + 1) * RET_DV)
        gn = gn_ref[hd]
        for t in range(n):
            rows = slice(t * c, (t + 1) * c)
            y = (_silu(gf_ref[rows, cols].astype(F32)) * _head_norm(of_scr[rows, cols], gn)
                 + _silu(gb_ref[rows, cols].astype(F32)) * _head_norm(ob_scr[rows, cols], gn))
            y_ref[rows, cols] = y.astype(y_ref.dtype)


def _retention(z, dec_f, dec_b, gn_g, layer, seq_len, row0, n_seq, heads, in_place=(), states=None):
    has_state = states is not None
    blk0 = row0 // seq_len
    w = heads * RET_DV
    zspec = lambda off: pl.BlockSpec((seq_len, w), lambda b, h: (blk0 + b, off // w + h))
    pspec = pl.BlockSpec((None, heads, 1, RET_DV), lambda b, h: (layer, h, 0, 0))
    sspec = pl.BlockSpec((None, None, heads, RET_DK, RET_DV), lambda b, h: (b, layer, h, 0, 0))
    in_specs = [pspec, pspec, zspec(OFF_RQ), zspec(OFF_RK), zspec(OFF_RV), zspec(OFF_RGF), zspec(OFF_RGB), pspec]
    args = [dec_f, dec_b, z, z, z, z, z, gn_g.reshape(DEPTH, RET_HEADS, 1, RET_DV)]
    if has_state:
        in_specs += [sspec, sspec]
        args += list(states)
    aliases = {}
    for k, prev in enumerate(in_place):
        if prev is not None:
            aliases[len(args)] = k
            in_specs.append(pl.BlockSpec(memory_space=pl.ANY))
            args.append(prev)
    y_spec = pl.BlockSpec((seq_len, w), lambda b, h: (blk0 + b, h))
    y_shape = jax.ShapeDtypeStruct((T_ALL, RET_V_DIM), BF16)
    st_bytes = heads * RET_DK * RET_DV * 4
    if has_state:
        out_specs, out_shape = y_spec, y_shape
    else:
        oshape = jax.ShapeDtypeStruct((n_seq, DEPTH, RET_HEADS, RET_DK, RET_DV), F32)
        out_specs, out_shape = [y_spec, sspec, sspec], [y_shape, oshape, oshape]
    return pl.pallas_call(
        functools.partial(_retention_kernel, seq_len, heads, has_state, len(aliases)),
        grid=(n_seq, RET_HEADS // heads),
        in_specs=in_specs,
        out_specs=out_specs,
        out_shape=out_shape,
        input_output_aliases=aliases,
        scratch_shapes=[pltpu.VMEM((seq_len, w), F32), pltpu.VMEM((seq_len, w), F32)],
        compiler_params=_params(("arbitrary", "arbitrary"),
                                _vmem_limit([seq_len * w * 2] * 6 + [st_bytes] * 2,
                                            scratch_bytes=2 * seq_len * w * 4, temp_bytes=8 << 20)),
        name="retention",
    )(*args)


def _merge_kernel(yc_ref, ya_ref, yr_ref, wc_ref, wa_ref, wr_ref, ga_ref, gb_ref, gr_ref, o_ref):
    def term(y_ref, w_ref, gate_ref):
        proj = jnp.dot(y_ref[...], w_ref[...], preferred_element_type=F32)
        return jax.nn.sigmoid(gate_ref[...].astype(F32)) * proj

    merged = term(yc_ref, wc_ref, ga_ref) + term(ya_ref, wa_ref, gb_ref) + term(yr_ref, wr_ref, gr_ref)
    o_ref[...] = merged.astype(o_ref.dtype)


def _merge(y_conv, y_attn, y_ret, w_conv_out, w_attn_out, w_ret_out, z, layer):
    bn = BN_MERGE
    yspec = pl.BlockSpec((BM, CONV_DIM), lambda i, j: (i, 0))
    wspec = pl.BlockSpec((None, CONV_DIM, bn), lambda i, j: (layer, 0, j))
    gspec = lambda off: pl.BlockSpec((BM, bn), lambda i, j: (i, off // bn + j))
    return pl.pallas_call(
        _merge_kernel,
        grid=(T_ALL // BM, D_MODEL // bn),
        in_specs=[yspec, yspec, yspec, wspec, wspec, wspec, gspec(OFF_GA), gspec(OFF_GB), gspec(OFF_GR)],
        out_specs=pl.BlockSpec((BM, bn), lambda i, j: (i, j)),
        out_shape=jax.ShapeDtypeStruct((T_ALL, D_MODEL), BF16),
        compiler_params=_params(("arbitrary", "arbitrary"),
                                _vmem_limit([BM * CONV_DIM * 2] * 3 + [CONV_DIM * bn * 2] * 3 + [BM * bn * 2] * 4,
                                            temp_bytes=3 * BM * bn * 4)),
        name="merge",
    )(y_conv, y_attn, y_ret, w_conv_out, w_attn_out, w_ret_out, z, z, z)


def _wo_kernel(m_ref, w_ref, x_ref, g_ref, *rest):
    o_ref = rest[-1]
    acc = jnp.dot(m_ref[...], w_ref[...], preferred_element_type=F32)
    o_ref[...] = x_ref[...] + g_ref[...] * acc


def _out_proj(merged, w_o, x, mod4, layer, row0=0, in_place=None):
    blk0 = row0 // BM
    per = D_MODEL // BN
    extra = [] if in_place is None else [in_place]
    return pl.pallas_call(
        _wo_kernel,
        grid=(x.shape[0] // BM, D_MODEL // BN),
        in_specs=[pl.BlockSpec((BM, D_MODEL), lambda i, j: (blk0 + i, 0)),
                  pl.BlockSpec((D_MODEL, BN), lambda i, j: (0, j)),
                  pl.BlockSpec((BM, BN), lambda i, j: (i, j)),
                  pl.BlockSpec((None, None, 1, BN),
                               lambda i, j: (layer, _cond_row(blk0 + i, BM), 0, 2 * per + j))]
                 + [pl.BlockSpec(memory_space=pl.ANY)] * len(extra),
        out_specs=pl.BlockSpec((BM, BN), lambda i, j: (blk0 + i, j)),
        out_shape=jax.ShapeDtypeStruct((T_ALL, D_MODEL), F32),
        input_output_aliases={4: 0} if extra else {},
        compiler_params=_params(("arbitrary", "arbitrary"),
                                _vmem_limit([BM * D_MODEL * 2, D_MODEL * BN * 2, BM * BN * 4, BM * BN * 4],
                                            temp_bytes=2 * BM * BN * 4)),
        name="out_proj",
    )(merged, w_o, x, mod4, *extra)


def _relu2_epilogue(acc, b_ref):
    r = jnp.maximum(acc + b_ref[...], 0.0)
    return r * r


def _ff1(h, w_ff1, b_ff1, layer):
    return _ws_matmul(h, w_ff1, layer, _relu2_epilogue, extra=(b_ff1.reshape(DEPTH, 1, D_FF),), name="ff1")


def _ff2_kernel(f_ref, w_ref, b_ref, x_ref, g_ref, o_ref, acc_ref):
    k = pl.program_id(2)

    @pl.when(k == 0)
    def _():
        acc_ref[...] = jnp.zeros_like(acc_ref)

    acc_ref[...] += jnp.dot(f_ref[...], w_ref[...], preferred_element_type=F32)

    @pl.when(k == pl.num_programs(2) - 1)
    def _():
        o_ref[...] = x_ref[...] + g_ref[...] * (acc_ref[...] + b_ref[...])


def _ff2(f, w_ff2, b_ff2, x, mod4, layer):
    bk = BK_FF2
    per = D_MODEL // BN
    return pl.pallas_call(
        _ff2_kernel,
        grid=(T_ALL // BM, D_MODEL // BN, D_FF // bk),
        in_specs=[pl.BlockSpec((BM, bk), lambda i, j, k: (i, k)),
                  pl.BlockSpec((bk, BN), lambda i, j, k: (k, j)),
                  pl.BlockSpec((None, 1, BN), lambda i, j, k: (layer, 0, j)),
                  pl.BlockSpec((BM, BN), lambda i, j, k: (i, j)),
                  pl.BlockSpec((None, None, 1, BN), lambda i, j, k: (layer, _cond_row(i, BM), 0, 5 * per + j))],
        out_specs=pl.BlockSpec((BM, BN), lambda i, j, k: (i, j)),
        out_shape=jax.ShapeDtypeStruct((T_ALL, D_MODEL), F32),
        scratch_shapes=[pltpu.VMEM((BM, BN), F32)],
        compiler_params=_params(("arbitrary", "arbitrary", "arbitrary"),
                                _vmem_limit([BM * bk * 2, bk * BN * 2, BM * BN * 4, BM * BN * 4],
                                            scratch_bytes=BM * BN * 4, temp_bytes=2 * BM * BN * 4)),
        name="ff2",
    )(f, w_ff2, b_ff2.reshape(DEPTH, 1, D_MODEL), x, mod4)


def _ret_param(p):
    return jnp.broadcast_to(p.astype(F32)[:, :, None, None], (DEPTH, RET_HEADS, 1, RET_DV))


def kernel(x_prompt, x_sample, c, cache_k, cache_v, state_ret_f, state_ret_b, c_ctx, w_mod, b_mod, norm1_g, w_in, conv_w, conv_b, attn_sink, ret_decay_f, ret_decay_b, ret_gn_g, w_conv_out, w_attn_out, w_ret_out, w_o, norm2_g, w_ff1, b_ff1, w_ff2, b_ff2, final_g):
    x_parts = ((x_prompt.reshape(T_CTX, D_MODEL), 0), (x_sample.reshape(T_LAT, D_MODEL), T_CTX))
    cond = jnp.concatenate([c_ctx[None, :], c, jnp.zeros((COND_PAD - N_COND, D_MODEL), F32)], axis=0)
    mod4 = _adaln(cond, w_mod, b_mod).reshape(DEPTH, COND_PAD, 1, N_MOD * D_MODEL)

    w_conv_b = w_conv_out.astype(BF16)
    w_attn_b = w_attn_out.astype(BF16)
    w_ret_b = w_ret_out.astype(BF16)
    dec_f = _ret_param(ret_decay_f)
    dec_b = _ret_param(ret_decay_b)
    cos, sin = _rope_tables()

    ks, vs = [], []
    new_states = ()
    for l in range(DEPTH):
        h = None
        for x_rows, row0 in x_parts:
            h = _norm_mod(x_rows, norm1_g, mod4, l, 0, row0, in_place=h)
        z, w_ff2_b, w_o_b = _in_proj(h, w_in, l, side=(w_ff2, w_o))
        kv = _kv_proj(h, w_in, l)
        ks.append(kv[:, :KV_DIM].reshape(BATCH, SEQ, N_KV_HEADS, HEAD_DIM))
        vs.append(kv[:, KV_DIM:].reshape(BATCH, SEQ, N_KV_HEADS, HEAD_DIM))

        y_conv = _conv_branch(z, conv_w, conv_b, l)
        y_attn = _lat_attention(_rope_qk(z, cos, sin), z, cache_k, cache_v, attn_sink, l,
                                _ctx_attention(z, attn_sink, l))
        y_ret, *new_states = _retention(z, dec_f, dec_b, ret_gn_g, l, SEQ, 0, BATCH, RET_HEADS_PER_STEP_CTX,
                                        in_place=(None,) + tuple(new_states) if new_states else ())
        y_ret = _retention(z, dec_f, dec_b, ret_gn_g, l, DEC_SEQ, T_CTX, DEC_BATCH, RET_HEADS_PER_STEP_LAT,
                           in_place=(y_ret,), states=(state_ret_f, state_ret_b))

        merged = _merge(y_conv, y_attn, y_ret, w_conv_b, w_attn_b, w_ret_b, z, l)
        x = None
        for x_rows, row0 in x_parts:
            x = _out_proj(merged, w_o_b, x_rows, mod4, l, row0, in_place=x)
        h = _norm_mod(x, norm2_g, mod4, l, 3)
        f = _ff1(h, w_ff1, b_ff1, l)
        x = _ff2(f, w_ff2_b, b_ff2, x, mod4, l)
        x_parts = ((x, 0),)

    y_prompt = _final_norm(x, final_g, 0, T_CTX).reshape(BATCH, SEQ, D_MODEL)
    y_sample = _final_norm(x, final_g, T_CTX, T_LAT).reshape(DEC_BATCH, DEC_SEQ, D_MODEL)
    return (y_prompt, y_sample, jnp.stack(ks, axis=1), jnp.stack(vs, axis=1), new_states[0], new_states[1])
```

```python
import functools

import numpy as np
import jax
import jax.numpy as jnp
from jax import lax
from jax.experimental import pallas as pl
from jax.experimental.pallas import tpu as pltpu

F32 = jnp.float32
BF16 = jnp.bfloat16
LOG2_E = float(np.log2(np.e))

D_MODEL = 4096
BATCH = 16
SEQ = 256
DEPTH = 2
DEC_BATCH = 8
DEC_SEQ = 1024
PAST_LEN = 256
GRID_W = 64
EPS = 1e-6
N_MOD = 6
CONV_DIM = 2048
N_Q_HEADS = 16
N_KV_HEADS = 4
GROUP = N_Q_HEADS // N_KV_HEADS
HEAD_DIM = 128
ATTN_DIM = N_Q_HEADS * HEAD_DIM
KV_DIM = N_KV_HEADS * HEAD_DIM
WINDOW = 128
BLOCK = 128
ROPE_BASE = 10000.0
RET_HEADS = 8
RET_DK = 256
RET_DV = 256
RET_V_DIM = RET_HEADS * RET_DV
RET_CHUNK = 128
D_FF = 4 * D_MODEL

T_CTX = BATCH * SEQ
T_LAT = DEC_BATCH * DEC_SEQ
T_ALL = T_CTX + T_LAT
N_COND = 1 + DEC_BATCH
COND_PAD = 16

OFF_CB = 0
OFF_CC = OFF_CB + CONV_DIM
OFF_CX = OFF_CC + CONV_DIM
OFF_Q = OFF_CX + CONV_DIM
OFF_K = OFF_Q + ATTN_DIM
OFF_V = OFF_K + KV_DIM
OFF_RQ = OFF_V + KV_DIM
OFF_RK = OFF_RQ + RET_HEADS * RET_DK
OFF_RV = OFF_RK + RET_HEADS * RET_DK
OFF_RGF = OFF_RV + RET_V_DIM
OFF_RGB = OFF_RGF + RET_V_DIM
OFF_GA = OFF_RGB + RET_V_DIM
OFF_GB = OFF_GA + D_MODEL
OFF_GR = OFF_GB + D_MODEL
IN_COLS = OFF_GR + D_MODEL

V7X_VMEM_LIMIT_CAP = 60000 * 1024
VMEM_BOOKKEEPING = 2 * 1024 * 1024
BM = 1024
BN = 1024
BK_FF2 = 2048
BM_NORM = 256
NORM_ROWS = 16
BN_MERGE = 512
BN_ADALN = 512
BN_CONV = 512
RET_HEADS_PER_STEP_CTX = 4
RET_HEADS_PER_STEP_LAT = 2
KC_WS = 512
SIDE_ROWS = 64
BN_KV = 512
assert OFF_K % BN_KV == 0 and OFF_V == OFF_K + KV_DIM
assert BM == DEC_SEQ and T_CTX % BM == 0 and BM % SEQ == 0
assert RET_DK == 4 ** 4


def _vmem_limit(block_bytes, scratch_bytes=0, temp_bytes=0):
    need = 2 * sum(block_bytes) + scratch_bytes + temp_bytes + VMEM_BOOKKEEPING
    return int(min(need, V7X_VMEM_LIMIT_CAP))


def _params(semantics, vmem):
    return pltpu.CompilerParams(dimension_semantics=semantics, vmem_limit_bytes=vmem)


def _cond_row(i, bm):
    n_ctx = T_CTX // bm
    per_seq = DEC_SEQ // bm
    return jnp.where(i < n_ctx, 0, 1 + (i - n_ctx) // per_seq)


def _adaln_kernel(c_ref, w_ref, b_ref, o_ref):
    s = c_ref[...]
    s = s * jax.nn.sigmoid(s)
    acc = jnp.dot(s.astype(BF16), w_ref[...].astype(BF16), preferred_element_type=F32)
    o_ref[...] = acc + b_ref[...]


def _adaln(cond, w_mod, b_mod):
    n = N_MOD * D_MODEL
    return pl.pallas_call(
        _adaln_kernel,
        grid=(DEPTH, n // BN_ADALN),
        in_specs=[pl.BlockSpec((COND_PAD, D_MODEL), lambda l, j: (0, 0)),
                  pl.BlockSpec((None, D_MODEL, BN_ADALN), lambda l, j: (l, 0, j)),
                  pl.BlockSpec((None, 1, BN_ADALN), lambda l, j: (l, 0, j))],
        out_specs=pl.BlockSpec((None, COND_PAD, BN_ADALN), lambda l, j: (l, 0, j)),
        out_shape=jax.ShapeDtypeStruct((DEPTH, COND_PAD, n), F32),
        compiler_params=_params(("arbitrary", "arbitrary"),
                                _vmem_limit([D_MODEL * BN_ADALN * 4], temp_bytes=D_MODEL * BN_ADALN * 2)),
        name="adaln",
    )(cond, w_mod, b_mod.reshape(DEPTH, 1, n))


def _rms(x, g):
    y = x * lax.rsqrt(jnp.mean(x * x, axis=-1, keepdims=True) + EPS)
    return y * g


def _norm_mod_kernel(x_ref, g_ref, sh_ref, sc_ref, *rest):
    o_ref = rest[-1]
    g = g_ref[...]
    scale1 = 1 + sc_ref[...]
    shift = sh_ref[...]
    for r in range(0, BM_NORM, NORM_ROWS):
        rows = slice(r, r + NORM_ROWS)
        o_ref[rows, :] = (_rms(x_ref[rows, :], g) * scale1 + shift).astype(o_ref.dtype)


def _norm_mod(x, g, mod4, layer, which_shift, row0=0, in_place=None):
    blk0 = row0 // BM_NORM
    mspec = lambda which: pl.BlockSpec((None, None, 1, D_MODEL),
                                       lambda i: (layer, _cond_row(blk0 + i, BM_NORM), 0, which))
    extra = [] if in_place is None else [in_place]
    return pl.pallas_call(
        _norm_mod_kernel,
        grid=(x.shape[0] // BM_NORM,),
        in_specs=[pl.BlockSpec((BM_NORM, D_MODEL), lambda i: (i, 0)),
                  pl.BlockSpec((None, 1, D_MODEL), lambda i: (layer, 0, 0)),
                  mspec(which_shift), mspec(which_shift + 1)]
                 + [pl.BlockSpec(memory_space=pl.ANY)] * len(extra),
        out_specs=pl.BlockSpec((BM_NORM, D_MODEL), lambda i: (blk0 + i, 0)),
        out_shape=jax.ShapeDtypeStruct((T_ALL, D_MODEL), BF16),
        input_output_aliases={4: 0} if extra else {},
        compiler_params=_params(("arbitrary",), _vmem_limit([BM_NORM * D_MODEL * 4, BM_NORM * D_MODEL * 2],
                                                            temp_bytes=2 * BM_NORM * D_MODEL * 4)),
        name="norm_mod",
    )(x, g.reshape(DEPTH, 1, D_MODEL), mod4, mod4, *extra)


def _norm_kernel(x_ref, g_ref, o_ref):
    g = g_ref[...]
    for r in range(0, BM_NORM, NORM_ROWS):
        rows = slice(r, r + NORM_ROWS)
        o_ref[rows, :] = _rms(x_ref[rows, :], g)


def _final_norm(x, g, row0, rows):
    blk0 = row0 // BM_NORM
    return pl.pallas_call(
        _norm_kernel,
        grid=(rows // BM_NORM,),
        in_specs=[pl.BlockSpec((BM_NORM, D_MODEL), lambda i: (blk0 + i, 0)),
                  pl.BlockSpec((1, D_MODEL), lambda i: (0, 0))],
        out_specs=pl.BlockSpec((BM_NORM, D_MODEL), lambda i: (i, 0)),
        out_shape=jax.ShapeDtypeStruct((rows, D_MODEL), F32),
        compiler_params=_params(("arbitrary",), _vmem_limit([BM_NORM * D_MODEL * 4] * 2,
                                                            temp_bytes=2 * BM_NORM * D_MODEL * 4)),
        name="final_norm",
    )(x, g.reshape(1, D_MODEL))


def _ws_matmul_kernel(epilogue, side_blocks, a_ref, wc_ref, *rest):
    n_side = len(side_blocks)
    n_extra = len(rest) - 2 * n_side - 2
    extra = rest[:n_extra]
    side_in = rest[n_extra:n_extra + n_side]
    o_ref = rest[n_extra + n_side]
    side_out = rest[n_extra + n_side + 1:-1]
    wbf_scr = rest[-1]
    jj = pl.program_id(0)
    i = pl.program_id(1)
    n_col = pl.num_programs(0) - 1
    n_chunk = wbf_scr.shape[1] // KC_WS

    step = jj * pl.num_programs(1) + i
    for (first, count), src, dst in zip(side_blocks, side_in, side_out):
        @pl.when(jnp.logical_and(step >= first, step < first + count))
        def _(src=src, dst=dst):
            dst[...] = src[...].astype(BF16)

    @pl.when(jnp.logical_and(jj < n_col, i < n_chunk))
    def _():
        r0 = pl.multiple_of(i * KC_WS, KC_WS)
        wbf_scr[jj % 2, pl.ds(r0, KC_WS), :] = wc_ref[...].astype(BF16)

    @pl.when(jj > 0)
    def _():
        acc = jnp.dot(a_ref[...], wbf_scr[(jj - 1) % 2], preferred_element_type=F32)
        o_ref[...] = epilogue(acc, *extra).astype(o_ref.dtype)


def _ws_matmul(a, w, layer, epilogue, extra=(), side=(), name="ws_matmul"):
    k, n = w.shape[1:]
    n_col = n // BN
    n_row = T_ALL // BM
    n_chunk = k // KC_WS
    assert n_row >= n_chunk and k % KC_WS == 0 and n % BN == 0
    row = lambda jj, i: jnp.where(jj == 0, 0, i)
    col = lambda jj: jnp.maximum(jj - 1, 0)

    side_blocks, side_in, side_out, side_shape = [], [], [], []
    first = 0
    for s in side:
        count = s.shape[1] // SIDE_ROWS
        assert s.shape[1] % SIDE_ROWS == 0 and s.shape[2] == D_MODEL
        blk = lambda jj, i, first=first, count=count: jnp.clip(jj * n_row + i - first, 0, count - 1)
        side_blocks.append((first, count))
        side_in.append(pl.BlockSpec((None, SIDE_ROWS, D_MODEL), lambda jj, i, blk=blk: (layer, blk(jj, i), 0)))
        side_out.append(pl.BlockSpec((SIDE_ROWS, D_MODEL), lambda jj, i, blk=blk: (blk(jj, i), 0)))
        side_shape.append(jax.ShapeDtypeStruct((s.shape[1], D_MODEL), BF16))
        first += count
    assert first <= (n_col + 1) * n_row

    out = pl.pallas_call(
        functools.partial(_ws_matmul_kernel, epilogue, tuple(side_blocks)),
        grid=(n_col + 1, n_row),
        in_specs=[pl.BlockSpec((BM, k), lambda jj, i: (row(jj, i), 0)),
                  pl.BlockSpec((None, KC_WS, BN),
                               lambda jj, i: (layer, jnp.minimum(i, n_chunk - 1), jnp.minimum(jj, n_col - 1)))]
                 + [pl.BlockSpec((None, 1, BN), lambda jj, i: (layer, 0, col(jj)))] * len(extra) + side_in,
        out_specs=[pl.BlockSpec((BM, BN), lambda jj, i: (row(jj, i), col(jj)))] + side_out,
        out_shape=[jax.ShapeDtypeStruct((T_ALL, n), BF16)] + side_shape,
        scratch_shapes=[pltpu.VMEM((2, k, BN), BF16)],
        compiler_params=_params(("arbitrary", "arbitrary"),
                                _vmem_limit([BM * k * 2, KC_WS * BN * 4, BM * BN * 2]
                                            + [SIDE_ROWS * D_MODEL * 6] * len(side),
                                            scratch_bytes=2 * k * BN * 2, temp_bytes=2 * BM * BN * 4)),
        name=name,
    )(a, w, *extra, *side)
    return out if side else out[0]


def _in_proj(h, w_in, layer, side):
    return _ws_matmul(h, w_in, layer, lambda acc: acc, side=side, name="in_proj")


def _kv_proj_kernel(h_ref, w_ref, o_ref):
    o_ref[...] = jnp.dot(h_ref[...], w_ref[...].astype(BF16), preferred_element_type=F32)


def _kv_proj(h, w_in, layer):
    return pl.pallas_call(
        _kv_proj_kernel,
        grid=(2 * KV_DIM // BN_KV, T_CTX // BM),
        in_specs=[pl.BlockSpec((BM, D_MODEL), lambda j, i: (i, 0)),
                  pl.BlockSpec((None, D_MODEL, BN_KV), lambda j, i: (layer, 0, OFF_K // BN_KV + j))],
        out_specs=pl.BlockSpec((BM, BN_KV), lambda j, i: (i, j)),
        out_shape=jax.ShapeDtypeStruct((T_CTX, 2 * KV_DIM), F32),
        compiler_params=_params(("arbitrary", "arbitrary"),
                                _vmem_limit([BM * D_MODEL * 2, D_MODEL * BN_KV * 4, BM * BN_KV * 4],
                                            temp_bytes=D_MODEL * BN_KV * 2 + BM * BN_KV * 4)),
        name="kv_proj",
    )(h, w_in)


def _conv_kernel(cb_ref, cc_ref, cx_ref, w_ref, b_ref, o_ref):
    i = pl.program_id(0)
    seq = jnp.where(i < T_CTX // BM, SEQ, DEC_SEQ)
    u = cc_ref[...].astype(F32) * cx_ref[...].astype(F32)
    pos = lax.broadcasted_iota(jnp.int32, u.shape, 0) & (seq - 1)
    prev = jnp.where(pos == 0, 0.0, pltpu.roll(u, 1, 0))
    nxt = jnp.where(pos == seq - 1, 0.0, pltpu.roll(u, BM - 1, 0))
    w = w_ref[...]
    conv = w[0:1] * prev + w[1:2] * u + w[2:3] * nxt + b_ref[...]
    o_ref[...] = (cb_ref[...].astype(F32) * conv).astype(o_ref.dtype)


def _conv_branch(z, conv_w, conv_b, layer):
    nb = CONV_DIM // BN_CONV
    zspec = lambda off: pl.BlockSpec((BM, BN_CONV), lambda i, j: (i, off // BN_CONV + j))
    return pl.pallas_call(
        _conv_kernel,
        grid=(T_ALL // BM, nb),
        in_specs=[zspec(OFF_CB), zspec(OFF_CC), zspec(OFF_CX),
                  pl.BlockSpec((None, 3, BN_CONV), lambda i, j: (layer, 0, j)),
                  pl.BlockSpec((None, 1, BN_CONV), lambda i, j: (layer, 0, j))],
        out_specs=pl.BlockSpec((BM, BN_CONV), lambda i, j: (i, j)),
        out_shape=jax.ShapeDtypeStruct((T_ALL, CONV_DIM), BF16),
        compiler_params=_params(("arbitrary", "arbitrary"),
                                _vmem_limit([BM * BN_CONV * 2] * 4, temp_bytes=6 * BM * BN_CONV * 4)),
        name="conv_branch",
    )(z, z, z, conv_w, conv_b.reshape(DEPTH, 1, CONV_DIM))


def _sink_column(sink_ref, layer, kvh, rows_per_head):
    row = lax.broadcasted_iota(jnp.int32, (GROUP * rows_per_head, 1), 0)
    col = jnp.full(row.shape, sink_ref[layer, kvh * GROUP], F32)
    for g in range(1, GROUP):
        col = jnp.where(row >= g * rows_per_head, sink_ref[layer, kvh * GROUP + g], col)
    return col


def _stack_heads(q):
    return jnp.concatenate([q[:, g * HEAD_DIM:(g + 1) * HEAD_DIM] for g in range(GROUP)], axis=0)


def _qk(q, k):
    return lax.dot_general(q, k, (((1,), (1,)), ((), ())), preferred_element_type=F32)


def _ctx_attn_kernel(layer, sink_ref, q_ref, k_ref, v_ref, o_ref):
    qw = GROUP * HEAD_DIM
    for kvh in range(N_KV_HEADS):
        q4 = _stack_heads(q_ref[:, kvh * qw:(kvh + 1) * qw])
        t = _qk(q4, k_ref[:, kvh * HEAD_DIM:(kvh + 1) * HEAD_DIM]) * (HEAD_DIM ** -0.5 * LOG2_E)
        sink2 = _sink_column(sink_ref, layer, kvh, SEQ) * LOG2_E
        m = jnp.maximum(jnp.max(t, axis=-1, keepdims=True), sink2)
        e = jnp.exp2(t - m)
        inv = 1.0 / (jnp.sum(e, axis=-1, keepdims=True) + jnp.exp2(sink2 - m))
        o = jnp.dot(e.astype(BF16), v_ref[:, kvh * HEAD_DIM:(kvh + 1) * HEAD_DIM],
                    preferred_element_type=F32) * inv
        for g in range(GROUP):
            c0 = (kvh * GROUP + g) * HEAD_DIM
            o_ref[:, c0:c0 + HEAD_DIM] = o[g * SEQ:(g + 1) * SEQ].astype(o_ref.dtype)


def _ctx_attention(z, attn_sink, layer):
    return pl.pallas_call(
        functools.partial(_ctx_attn_kernel, layer),
        grid=(BATCH,),
        in_specs=[pl.BlockSpec(memory_space=pltpu.SMEM),
                  pl.BlockSpec((SEQ, ATTN_DIM), lambda b: (b, OFF_Q // ATTN_DIM)),
                  pl.BlockSpec((SEQ, KV_DIM), lambda b: (b, OFF_K // KV_DIM)),
                  pl.BlockSpec((SEQ, KV_DIM), lambda b: (b, OFF_V // KV_DIM))],
        out_specs=pl.BlockSpec((SEQ, ATTN_DIM), lambda b: (b, 0)),
        out_shape=jax.ShapeDtypeStruct((T_ALL, ATTN_DIM), BF16),
        compiler_params=_params(("arbitrary",),
                                _vmem_limit([SEQ * ATTN_DIM * 2] * 2 + [SEQ * KV_DIM * 2] * 2,
                                            temp_bytes=N_Q_HEADS * SEQ * SEQ * 4)),
        name="ctx_attention",
    )(attn_sink, z, z, z)


def _rope_tables():
    pos = np.arange(DEC_SEQ)
    row_id = jnp.asarray(pos // GRID_W, F32)
    col_id = jnp.asarray(pos % GRID_W, F32)
    half = HEAD_DIM // 2
    freqs = ROPE_BASE ** (-jnp.arange(0, half, 2, dtype=F32) / half)
    ang_r = row_id[:, None] * freqs[None, :]
    ang_c = col_id[:, None] * freqs[None, :]
    cos = jnp.concatenate([jnp.cos(ang_r), jnp.cos(ang_r), jnp.cos(ang_c), jnp.cos(ang_c)], axis=-1)
    sin = jnp.concatenate([-jnp.sin(ang_r), jnp.sin(ang_r), -jnp.sin(ang_c), jnp.sin(ang_c)], axis=-1)
    return cos, sin


def _rope_kernel(x_ref, cos_ref, sin_ref, o_ref):
    cos = cos_ref[...]
    sin = sin_ref[...]
    quarter = HEAD_DIM // 4
    lane = lax.broadcasted_iota(jnp.int32, cos.shape, 1)
    first = (lane % (2 * quarter)) < quarter
    for g in range(GROUP):
        x = x_ref[:, g * HEAD_DIM:(g + 1) * HEAD_DIM].astype(F32)
        partner = jnp.where(first, pltpu.roll(x, HEAD_DIM - quarter, 1), pltpu.roll(x, quarter, 1))
        o_ref[:, g * HEAD_DIM:(g + 1) * HEAD_DIM] = (x * cos + partner * sin).astype(o_ref.dtype)


def _rope_qk(z, cos, sin):
    w = GROUP * HEAD_DIM
    nb = (ATTN_DIM + KV_DIM) // w
    tab = pl.BlockSpec((DEC_SEQ, HEAD_DIM), lambda b, j: (0, 0))
    return pl.pallas_call(
        _rope_kernel,
        grid=(DEC_BATCH, nb),
        in_specs=[pl.BlockSpec((DEC_SEQ, w), lambda b, j: (T_CTX // DEC_SEQ + b, OFF_Q // w + j)), tab, tab],
        out_specs=pl.BlockSpec((DEC_SEQ, w), lambda b, j: (b, j)),
        out_shape=jax.ShapeDtypeStruct((T_LAT, ATTN_DIM + KV_DIM), BF16),
        compiler_params=_params(("arbitrary", "arbitrary"),
                                _vmem_limit([DEC_SEQ * w * 2] * 2 + [DEC_SEQ * HEAD_DIM * 4] * 2,
                                            temp_bytes=6 * DEC_SEQ * HEAD_DIM * 4)),
        name="rope_qk",
    )(z, cos, sin)


def _lat_attn_kernel(layer, sink_ref, q_ref, k_ref, v_ref, kc_ref, vc_ref, _, o_ref):
    kvh = pl.program_id(1)
    nblk = DEC_SEQ // BLOCK
    scale2 = HEAD_DIM ** -0.5 * LOG2_E
    sink2 = _sink_column(sink_ref, layer, kvh, BLOCK) * LOG2_E
    k_ctx = kc_ref[...].astype(BF16)
    v_ctx = vc_ref[...].astype(BF16)
    band_bias = {}

    def bias(width, q_off):
        if (width, q_off) not in band_bias:
            shape = (GROUP * BLOCK, width)
            qpos = q_off + (lax.broadcasted_iota(jnp.int32, shape, 0) & (BLOCK - 1))
            kpos = lax.broadcasted_iota(jnp.int32, shape, 1)
            band_bias[width, q_off] = jnp.where(jnp.abs(kpos - qpos) <= WINDOW, 0.0, -jnp.inf).astype(F32)
        return band_bias[width, q_off]

    for n in range(nblk):
        lo = max(n - 1, 0) * BLOCK
        hi = min(n + 2, nblk) * BLOCK
        q4 = _stack_heads(q_ref[n * BLOCK:(n + 1) * BLOCK, :])
        t_loc = _qk(q4, k_ref[lo:hi, :]) * scale2 + bias(hi - lo, n * BLOCK - lo)
        t_ctx = _qk(q4, k_ctx) * scale2
        m = jnp.maximum(jnp.maximum(jnp.max(t_loc, axis=-1, keepdims=True),
                                    jnp.max(t_ctx, axis=-1, keepdims=True)), sink2)
        e_loc = jnp.exp2(t_loc - m)
        e_ctx = jnp.exp2(t_ctx - m)
        inv = 1.0 / (jnp.sum(e_loc, axis=-1, keepdims=True) + jnp.sum(e_ctx, axis=-1, keepdims=True)
                     + jnp.exp2(sink2 - m))
        o = (jnp.dot(e_loc.astype(BF16), v_ref[lo:hi, :], preferred_element_type=F32)
             + jnp.dot(e_ctx.astype(BF16), v_ctx, preferred_element_type=F32)) * inv
        for g in range(GROUP):
            o_ref[n * BLOCK:(n + 1) * BLOCK, g * HEAD_DIM:(g + 1) * HEAD_DIM] = (
                o[g * BLOCK:(g + 1) * BLOCK].astype(o_ref.dtype))


def _lat_attention(qk_rot, z, cache_k, cache_v, attn_sink, layer, y_ctx):
    qw = GROUP * HEAD_DIM
    ck = cache_k.reshape(DEC_BATCH, DEPTH, PAST_LEN, KV_DIM)
    cv = cache_v.reshape(DEC_BATCH, DEPTH, PAST_LEN, KV_DIM)
    cspec = pl.BlockSpec((None, None, PAST_LEN, HEAD_DIM), lambda b, h: (b, layer, 0, h))
    return pl.pallas_call(
        functools.partial(_lat_attn_kernel, layer),
        grid=(DEC_BATCH, N_KV_HEADS),
        in_specs=[pl.BlockSpec(memory_space=pltpu.SMEM),
                  pl.BlockSpec((DEC_SEQ, qw), lambda b, h: (b, h)),
                  pl.BlockSpec((DEC_SEQ, HEAD_DIM), lambda b, h: (b, ATTN_DIM // HEAD_DIM + h)),
                  pl.BlockSpec((DEC_SEQ, HEAD_DIM), lambda b, h: (T_CTX // DEC_SEQ + b, OFF_V // HEAD_DIM + h)),
                  cspec, cspec, pl.BlockSpec(memory_space=pl.ANY)],
        out_specs=pl.BlockSpec((DEC_SEQ, qw), lambda b, h: (T_CTX // DEC_SEQ + b, h)),
        out_shape=jax.ShapeDtypeStruct((T_ALL, ATTN_DIM), BF16),
        input_output_aliases={6: 0},
        compiler_params=_params(("arbitrary", "arbitrary"),
                                _vmem_limit([DEC_SEQ * qw * 2] * 2 + [DEC_SEQ * HEAD_DIM * 2] * 2
                                            + [PAST_LEN * HEAD_DIM * 4] * 2,
                                            temp_bytes=DEC_SEQ * GROUP * (3 * BLOCK + PAST_LEN) * 4)),
        name="lat_attention",
    )(attn_sink, qk_rot, qk_rot, z, ck, cv, y_ctx)


def _log_sigmoid(x):
    return jnp.minimum(x, 0.0) - jnp.log(1.0 + jnp.exp(-jnp.abs(x)))


def _head_norm(o, g):
    mu = jnp.mean(o, axis=-1, keepdims=True)
    d = o - mu
    var = jnp.mean(d * d, axis=-1, keepdims=True)
    return d * lax.rsqrt(var + EPS) * g


def _silu(x):
    return x * jax.nn.sigmoid(x)


def _kt_v(k, v):
    return lax.dot_general(k, v, (((0,), (0,)), ((), ())), preferred_element_type=F32)


def _retention_kernel(seq_len, heads, has_state, n_alias, *refs):
    df_ref, db_ref, rq_ref, rk_ref, rv_ref, gf_ref, gb_ref, gn_ref = refs[:8]
    rest = refs[8:]
    if has_state:
        s0f_ref, s0b_ref = rest[:2]
        rest = rest[2:]
    rest = rest[n_alias:]
    if has_state:
        y_ref, of_scr, ob_scr = rest
    else:
        y_ref, sf_ref, sb_ref, of_scr, ob_scr = rest
    n = seq_len // RET_CHUNK
    c = RET_CHUNK
    ii = lax.broadcasted_iota(jnp.int32, (c, c), 0)
    jj = lax.broadcasted_iota(jnp.int32, (c, c), 1)
    row = lax.broadcasted_iota(jnp.int32, (c, RET_DV), 0).astype(F32)

    def direction(hd, lg, rel, q_pow, k_pow, s, order, o_scr):
        cols = slice(hd * RET_DV, (hd + 1) * RET_DV)
        keep = rel >= 0
        k_scale = RET_DK ** -0.5
        dmat = jnp.where(keep, jnp.exp(jnp.where(keep, rel, 0).astype(F32) * lg[:, :c]), 0.0) * k_scale
        q_dec = jnp.exp(q_pow * lg)
        k_dec = jnp.exp(k_pow * lg) * k_scale
        chunk_dec = jnp.exp(c * lg)
        for t in order:
            rows = slice(t * c, (t + 1) * c)
            qc = rq_ref[rows, cols]
            kc = rk_ref[rows, cols]
            vc = rv_ref[rows, cols]
            inner = _qk(qc, kc) * dmat
            o = jnp.dot(inner.astype(BF16), vc, preferred_element_type=F32)
            kv = _kt_v((kc.astype(F32) * k_dec).astype(BF16), vc)
            if s is None:
                s = kv
            else:
                o = o + jnp.dot(qc, s.astype(BF16), preferred_element_type=F32) * q_dec
                s = chunk_dec * s + kv
            o_scr[rows, cols] = o
        return s

    for hd in range(heads):
        lg_f = _log_sigmoid(df_ref[hd])
        lg_b = _log_sigmoid(db_ref[hd])
        s_f = direction(hd, lg_f, ii - jj, row + 1.0, c - 1.0 - row,
                        s0f_ref[hd] if has_state else None, range(n), of_scr)
        s_b = direction(hd, lg_b, jj - ii, c - row, row,
                        s0b_ref[hd] if has_state else None, range(n - 1, -1, -1), ob_scr)
        if not has_state:
            sf_ref[hd] = s_f
            sb_ref[hd] = s_b
        cols = slice(hd * RET_DV, (hd + 1) * RET_DV)
        gn = gn_ref[hd]
        for t in range(n):
            rows = slice(t * c, (t + 1) * c)
            y = (_silu(gf_ref[rows, cols].astype(F32)) * _head_norm(of_scr[rows, cols], gn)
                 + _silu(gb_ref[rows, cols].astype(F32)) * _head_norm(ob_scr[rows, cols], gn))
            y_ref[rows, cols] = y.astype(y_ref.dtype)


def _retention(z, dec_f, dec_b, gn_g, layer, seq_len, row0, n_seq, heads, in_place=(), states=None):
    has_state = states is not None
    blk0 = row0 // seq_len
    w = heads * RET_DV
    zspec = lambda off: pl.BlockSpec((seq_len, w), lambda b, h: (blk0 + b, off // w + h))
    pspec = pl.BlockSpec((None, heads, 1, RET_DV), lambda b, h: (layer, h, 0, 0))
    sspec = pl.BlockSpec((None, None, heads, RET_DK, RET_DV), lambda b, h: (b, layer, h, 0, 0))
    in_specs = [pspec, pspec, zspec(OFF_RQ), zspec(OFF_RK), zspec(OFF_RV), zspec(OFF_RGF), zspec(OFF_RGB), pspec]
    args = [dec_f, dec_b, z, z, z, z, z, gn_g.reshape(DEPTH, RET_HEADS, 1, RET_DV)]
    if has_state:
        in_specs += [sspec, sspec]
        args += list(states)
    aliases = {}
    for k, prev in enumerate(in_place):
        if prev is not None:
            aliases[len(args)] = k
            in_specs.append(pl.BlockSpec(memory_space=pl.ANY))
            args.append(prev)
    y_spec = pl.BlockSpec((seq_len, w), lambda b, h: (blk0 + b, h))
    y_shape = jax.ShapeDtypeStruct((T_ALL, RET_V_DIM), BF16)
    st_bytes = heads * RET_DK * RET_DV * 4
    if has_state:
        out_specs, out_shape = y_spec, y_shape
    else:
        oshape = jax.ShapeDtypeStruct((n_seq, DEPTH, RET_HEADS, RET_DK, RET_DV), F32)
        out_specs, out_shape = [y_spec, sspec, sspec], [y_shape, oshape, oshape]
    return pl.pallas_call(
        functools.partial(_retention_kernel, seq_len, heads, has_state, len(aliases)),
        grid=(n_seq, RET_HEADS // heads),
        in_specs=in_specs,
        out_specs=out_specs,
        out_shape=out_shape,
        input_output_aliases=aliases,
        scratch_shapes=[pltpu.VMEM((seq_len, w), F32), pltpu.VMEM((seq_len, w), F32)],
        compiler_params=_params(("arbitrary", "arbitrary"),
                                _vmem_limit([seq_len * w * 2] * 6 + [st_bytes] * 2,
                                            scratch_bytes=2 * seq_len * w * 4,
                                            temp_bytes=2 * (seq_len * w * 4 + st_bytes))),
        name="retention",
    )(*args)


def _merge_kernel(yc_ref, ya_ref, yr_ref, wc_ref, wa_ref, wr_ref, ga_ref, gb_ref, gr_ref, o_ref):
    def term(y_ref, w_ref, gate_ref):
        proj = jnp.dot(y_ref[...], w_ref[...], preferred_element_type=F32)
        return jax.nn.sigmoid(gate_ref[...].astype(F32)) * proj

    merged = term(yc_ref, wc_ref, ga_ref) + term(ya_ref, wa_ref, gb_ref) + term(yr_ref, wr_ref, gr_ref)
    o_ref[...] = merged.astype(o_ref.dtype)


def _merge(y_conv, y_attn, y_ret, w_conv_out, w_attn_out, w_ret_out, z, layer):
    bn = BN_MERGE
    yspec = pl.BlockSpec((BM, CONV_DIM), lambda i, j: (i, 0))
    wspec = pl.BlockSpec((None, CONV_DIM, bn), lambda i, j: (layer, 0, j))
    gspec = lambda off: pl.BlockSpec((BM, bn), lambda i, j: (i, off // bn + j))
    return pl.pallas_call(
        _merge_kernel,
        grid=(T_ALL // BM, D_MODEL // bn),
        in_specs=[yspec, yspec, yspec, wspec, wspec, wspec, gspec(OFF_GA), gspec(OFF_GB), gspec(OFF_GR)],
        out_specs=pl.BlockSpec((BM, bn), lambda i, j: (i, j)),
        out_shape=jax.ShapeDtypeStruct((T_ALL, D_MODEL), BF16),
        compiler_params=_params(("arbitrary", "arbitrary"),
                                _vmem_limit([BM * CONV_DIM * 2] * 3 + [CONV_DIM * bn * 2] * 3 + [BM * bn * 2] * 4,
                                            temp_bytes=3 * BM * bn * 4)),
        name="merge",
    )(y_conv, y_attn, y_ret, w_conv_out, w_attn_out, w_ret_out, z, z, z)


def _wo_kernel(m_ref, w_ref, x_ref, g_ref, *rest):
    o_ref = rest[-1]
    acc = jnp.dot(m_ref[...], w_ref[...], preferred_element_type=F32)
    o_ref[...] = x_ref[...] + g_ref[...] * acc


def _out_proj(merged, w_o, x, mod4, layer, row0=0, in_place=None):
    blk0 = row0 // BM
    per = D_MODEL // BN
    extra = [] if in_place is None else [in_place]
    return pl.pallas_call(
        _wo_kernel,
        grid=(x.shape[0] // BM, D_MODEL // BN),
        in_specs=[pl.BlockSpec((BM, D_MODEL), lambda i, j: (blk0 + i, 0)),
                  pl.BlockSpec((D_MODEL, BN), lambda i, j: (0, j)),
                  pl.BlockSpec((BM, BN), lambda i, j: (i, j)),
                  pl.BlockSpec((None, None, 1, BN),
                               lambda i, j: (layer, _cond_row(blk0 + i, BM), 0, 2 * per + j))]
                 + [pl.BlockSpec(memory_space=pl.ANY)] * len(extra),
        out_specs=pl.BlockSpec((BM, BN), lambda i, j: (blk0 + i, j)),
        out_shape=jax.ShapeDtypeStruct((T_ALL, D_MODEL), F32),
        input_output_aliases={4: 0} if extra else {},
        compiler_params=_params(("arbitrary", "arbitrary"),
                                _vmem_limit([BM * D_MODEL * 2, D_MODEL * BN * 2, BM * BN * 4, BM * BN * 4],
                                            temp_bytes=2 * BM * BN * 4)),
        name="out_proj",
    )(merged, w_o, x, mod4, *extra)


def _relu2_epilogue(acc, b_ref):
    r = jnp.maximum(acc + b_ref[...], 0.0)
    return r * r


def _ff1(h, w_ff1, b_ff1, layer):
    return _ws_matmul(h, w_ff1, layer, _relu2_epilogue, extra=(b_ff1.reshape(DEPTH, 1, D_FF),), name="ff1")


def _ff2_kernel(f_ref, w_ref, b_ref, x_ref, g_ref, o_ref, acc_ref):
    k = pl.program_id(2)
    last = pl.num_programs(2) - 1

    def part():
        return jnp.dot(f_ref[...], w_ref[...], preferred_element_type=F32)

    @pl.when(k == 0)
    def _():
        acc_ref[...] = part()

    @pl.when(jnp.logical_and(k > 0, k < last))
    def _():
        acc_ref[...] += part()

    @pl.when(k == last)
    def _():
        o_ref[...] = x_ref[...] + g_ref[...] * ((acc_ref[...] + part()) + b_ref[...])


def _ff2(f, w_ff2, b_ff2, x, mod4, layer):
    bk = BK_FF2
    per = D_MODEL // BN
    assert D_FF // bk >= 2
    return pl.pallas_call(
        _ff2_kernel,
        grid=(T_ALL // BM, D_MODEL // BN, D_FF // bk),
        in_specs=[pl.BlockSpec((BM, bk), lambda i, j, k: (i, k)),
                  pl.BlockSpec((bk, BN), lambda i, j, k: (k, j)),
                  pl.BlockSpec((None, 1, BN), lambda i, j, k: (layer, 0, j)),
                  pl.BlockSpec((BM, BN), lambda i, j, k: (i, j)),
                  pl.BlockSpec((None, None, 1, BN), lambda i, j, k: (layer, _cond_row(i, BM), 0, 5 * per + j))],
        out_specs=pl.BlockSpec((BM, BN), lambda i, j, k: (i, j)),
        out_shape=jax.ShapeDtypeStruct((T_ALL, D_MODEL), F32),
        scratch_shapes=[pltpu.VMEM((BM, BN), F32)],
        compiler_params=_params(("arbitrary", "arbitrary", "arbitrary"),
                                _vmem_limit([BM * bk * 2, bk * BN * 2, BM * BN * 4, BM * BN * 4],
                                            scratch_bytes=BM * BN * 4, temp_bytes=2 * BM * BN * 4)),
        name="ff2",
    )(f, w_ff2, b_ff2.reshape(DEPTH, 1, D_MODEL), x, mod4)


def _ret_param(p):
    return jnp.broadcast_to(p.astype(F32)[:, :, None, None], (DEPTH, RET_HEADS, 1, RET_DV))


def kernel(x_prompt, x_sample, c, cache_k, cache_v, state_ret_f, state_ret_b, c_ctx, w_mod, b_mod, norm1_g, w_in, conv_w, conv_b, attn_sink, ret_decay_f, ret_decay_b, ret_gn_g, w_conv_out, w_attn_out, w_ret_out, w_o, norm2_g, w_ff1, b_ff1, w_ff2, b_ff2, final_g):
    x_parts = ((x_prompt.reshape(T_CTX, D_MODEL), 0), (x_sample.reshape(T_LAT, D_MODEL), T_CTX))
    cond = jnp.concatenate([c_ctx[None, :], c, jnp.zeros((COND_PAD - N_COND, D_MODEL), F32)], axis=0)
    mod4 = _adaln(cond, w_mod, b_mod).reshape(DEPTH, COND_PAD, 1, N_MOD * D_MODEL)

    w_conv_b = w_conv_out.astype(BF16)
    w_attn_b = w_attn_out.astype(BF16)
    w_ret_b = w_ret_out.astype(BF16)
    dec_f = _ret_param(ret_decay_f)
    dec_b = _ret_param(ret_decay_b)
    cos, sin = _rope_tables()

    ks, vs = [], []
    new_states = ()
    for l in range(DEPTH):
        h = None
        for x_rows, row0 in x_parts:
            h = _norm_mod(x_rows, norm1_g, mod4, l, 0, row0, in_place=h)
        z, w_ff2_b, w_o_b = _in_proj(h, w_in, l, side=(w_ff2, w_o))
        kv = _kv_proj(h, w_in, l)
        ks.append(kv[:, :KV_DIM].reshape(BATCH, SEQ, N_KV_HEADS, HEAD_DIM))
        vs.append(kv[:, KV_DIM:].reshape(BATCH, SEQ, N_KV_HEADS, HEAD_DIM))

        y_conv = _conv_branch(z, conv_w, conv_b, l)
        y_attn = _lat_attention(_rope_qk(z, cos, sin), z, cache_k, cache_v, attn_sink, l,
                                _ctx_attention(z, attn_sink, l))
        y_ret, *new_states = _retention(z, dec_f, dec_b, ret_gn_g, l, SEQ, 0, BATCH, RET_HEADS_PER_STEP_CTX,
                                        in_place=(None,) + tuple(new_states) if new_states else ())
        y_ret = _retention(z, dec_f, dec_b, ret_gn_g, l, DEC_SEQ, T_CTX, DEC_BATCH, RET_HEADS_PER_STEP_LAT,
                           in_place=(y_ret,), states=(state_ret_f, state_ret_b))

        merged = _merge(y_conv, y_attn, y_ret, w_conv_b, w_attn_b, w_ret_b, z, l)
        x = None
        for x_rows, row0 in x_parts:
            x = _out_proj(merged, w_o_b, x_rows, mod4, l, row0, in_place=x)
        h = _norm_mod(x, norm2_g, mod4, l, 3)
        f = _ff1(h, w_ff1, b_ff1, l)
        x = _ff2(f, w_ff2_b, b_ff2, x, mod4, l)
        x_parts = ((x, 0),)

    y_prompt = _final_norm(x, final_g, 0, T_CTX).reshape(BATCH, SEQ, D_MODEL)
    y_sample = _final_norm(x, final_g, T_CTX, T_LAT).reshape(DEC_BATCH, DEC_SEQ, D_MODEL)
    return (y_prompt, y_sample, jnp.stack(ks, axis=1), jnp.stack(vs, axis=1), new_states[0], new_states[1])
```

```python
import functools

import numpy as np
import jax
import jax.numpy as jnp
from jax import lax
from jax.experimental import pallas as pl
from jax.experimental.pallas import tpu as pltpu

F32 = jnp.float32
BF16 = jnp.bfloat16
LOG2_E = float(np.log2(np.e))

D_MODEL = 4096
BATCH = 16
SEQ = 256
DEPTH = 2
DEC_BATCH = 8
DEC_SEQ = 1024
PAST_LEN = 256
GRID_W = 64
EPS = 1e-6
N_MOD = 6
CONV_DIM = 2048
N_Q_HEADS = 16
N_KV_HEADS = 4
GROUP = N_Q_HEADS // N_KV_HEADS
HEAD_DIM = 128
ATTN_DIM = N_Q_HEADS * HEAD_DIM
KV_DIM = N_KV_HEADS * HEAD_DIM
WINDOW = 128
BLOCK = 128
ROPE_BASE = 10000.0
RET_HEADS = 8
RET_DK = 256
RET_DV = 256
RET_V_DIM = RET_HEADS * RET_DV
RET_CHUNK = 128
D_FF = 4 * D_MODEL

T_CTX = BATCH * SEQ
T_LAT = DEC_BATCH * DEC_SEQ
T_ALL = T_CTX + T_LAT
N_COND = 1 + DEC_BATCH
COND_PAD = 16

OFF_CB = 0
OFF_CC = OFF_CB + CONV_DIM
OFF_CX = OFF_CC + CONV_DIM
OFF_Q = OFF_CX + CONV_DIM
OFF_K = OFF_Q + ATTN_DIM
OFF_V = OFF_K + KV_DIM
OFF_RQ = OFF_V + KV_DIM
OFF_RK = OFF_RQ + RET_HEADS * RET_DK
OFF_RV = OFF_RK + RET_HEADS * RET_DK
OFF_RGF = OFF_RV + RET_V_DIM
OFF_RGB = OFF_RGF + RET_V_DIM
OFF_GA = OFF_RGB + RET_V_DIM
OFF_GB = OFF_GA + D_MODEL
OFF_GR = OFF_GB + D_MODEL
IN_COLS = OFF_GR + D_MODEL

V7X_VMEM_LIMIT_CAP = 60000 * 1024
VMEM_BOOKKEEPING = 2 * 1024 * 1024
BM = 1024
BN = 1024
BK_FF2 = 2048
BM_NORM = 256
NORM_ROWS = 16
BN_MERGE = 512
BN_ADALN = 512
BN_CONV = 512
RET_HEADS_PER_STEP_CTX = 4
RET_HEADS_PER_STEP_LAT = 4
KC_WS = 512
SIDE_ROWS = 64
BN_KV = 512
assert OFF_K % BN_KV == 0 and OFF_V == OFF_K + KV_DIM
assert BM == DEC_SEQ and T_CTX % BM == 0 and BM % SEQ == 0
assert RET_DK == 4 ** 4
assert all(off % BN_CONV == 0 for off in (OFF_CB, OFF_CC, OFF_CX))
assert all(off % BN_MERGE == 0 for off in (OFF_GA, OFF_GB, OFF_GR))
assert OFF_Q % ATTN_DIM == 0 and OFF_K % KV_DIM == 0 and OFF_V % KV_DIM == 0


def _vmem_limit(block_bytes, scratch_bytes=0, temp_bytes=0):
    need = 2 * sum(block_bytes) + scratch_bytes + temp_bytes + VMEM_BOOKKEEPING
    return int(min(need, V7X_VMEM_LIMIT_CAP))


def _params(semantics, vmem):
    return pltpu.CompilerParams(dimension_semantics=semantics, vmem_limit_bytes=vmem)


def _cond_row(i, bm):
    n_ctx = T_CTX // bm
    per_seq = DEC_SEQ // bm
    return jnp.where(i < n_ctx, 0, 1 + (i - n_ctx) // per_seq)


def _adaln_kernel(c_ref, w_ref, b_ref, o_ref):
    s = c_ref[...]
    s = s * jax.nn.sigmoid(s)
    acc = jnp.dot(s.astype(BF16), w_ref[...].astype(BF16), preferred_element_type=F32)
    o_ref[...] = acc + b_ref[...]


def _adaln(cond, w_mod, b_mod):
    n = N_MOD * D_MODEL
    return pl.pallas_call(
        _adaln_kernel,
        grid=(DEPTH, n // BN_ADALN),
        in_specs=[pl.BlockSpec((COND_PAD, D_MODEL), lambda l, j: (0, 0)),
                  pl.BlockSpec((None, D_MODEL, BN_ADALN), lambda l, j: (l, 0, j)),
                  pl.BlockSpec((None, 1, BN_ADALN), lambda l, j: (l, 0, j))],
        out_specs=pl.BlockSpec((None, COND_PAD, BN_ADALN), lambda l, j: (l, 0, j)),
        out_shape=jax.ShapeDtypeStruct((DEPTH, COND_PAD, n), F32),
        compiler_params=_params(("arbitrary", "arbitrary"),
                                _vmem_limit([D_MODEL * BN_ADALN * 4], temp_bytes=D_MODEL * BN_ADALN * 2)),
        name="adaln",
    )(cond, w_mod, b_mod.reshape(DEPTH, 1, n))


def _rms(x, g):
    y = x * lax.rsqrt(jnp.mean(x * x, axis=-1, keepdims=True) + EPS)
    return y * g


def _norm_mod_kernel(x_ref, g_ref, sh_ref, sc_ref, *rest):
    o_ref = rest[-1]
    g = g_ref[...]
    scale1 = 1 + sc_ref[...]
    shift = sh_ref[...]
    for r in range(0, BM_NORM, NORM_ROWS):
        rows = slice(r, r + NORM_ROWS)
        o_ref[rows, :] = (_rms(x_ref[rows, :], g) * scale1 + shift).astype(o_ref.dtype)


def _norm_mod(x, g, mod4, layer, which_shift, row0=0, in_place=None):
    blk0 = row0 // BM_NORM
    mspec = lambda which: pl.BlockSpec((None, None, 1, D_MODEL),
                                       lambda i: (layer, _cond_row(blk0 + i, BM_NORM), 0, which))
    extra = [] if in_place is None else [in_place]
    return pl.pallas_call(
        _norm_mod_kernel,
        grid=(x.shape[0] // BM_NORM,),
        in_specs=[pl.BlockSpec((BM_NORM, D_MODEL), lambda i: (i, 0)),
                  pl.BlockSpec((None, 1, D_MODEL), lambda i: (layer, 0, 0)),
                  mspec(which_shift), mspec(which_shift + 1)]
                 + [pl.BlockSpec(memory_space=pl.ANY)] * len(extra),
        out_specs=pl.BlockSpec((BM_NORM, D_MODEL), lambda i: (blk0 + i, 0)),
        out_shape=jax.ShapeDtypeStruct((T_ALL, D_MODEL), BF16),
        input_output_aliases={4: 0} if extra else {},
        compiler_params=_params(("arbitrary",), _vmem_limit([BM_NORM * D_MODEL * 4, BM_NORM * D_MODEL * 2],
                                                            temp_bytes=2 * BM_NORM * D_MODEL * 4)),
        name="norm_mod",
    )(x, g.reshape(DEPTH, 1, D_MODEL), mod4, mod4, *extra)


def _norm_kernel(x_ref, g_ref, o_ref):
    g = g_ref[...]
    for r in range(0, BM_NORM, NORM_ROWS):
        rows = slice(r, r + NORM_ROWS)
        o_ref[rows, :] = _rms(x_ref[rows, :], g)


def _final_norm(x, g, row0, rows):
    blk0 = row0 // BM_NORM
    return pl.pallas_call(
        _norm_kernel,
        grid=(rows // BM_NORM,),
        in_specs=[pl.BlockSpec((BM_NORM, D_MODEL), lambda i: (blk0 + i, 0)),
                  pl.BlockSpec((1, D_MODEL), lambda i: (0, 0))],
        out_specs=pl.BlockSpec((BM_NORM, D_MODEL), lambda i: (i, 0)),
        out_shape=jax.ShapeDtypeStruct((rows, D_MODEL), F32),
        compiler_params=_params(("arbitrary",), _vmem_limit([BM_NORM * D_MODEL * 4] * 2,
                                                            temp_bytes=2 * BM_NORM * D_MODEL * 4)),
        name="final_norm",
    )(x, g.reshape(1, D_MODEL))


def _ws_matmul_kernel(epilogue, side_blocks, a_ref, wc_ref, *rest):
    n_side = len(side_blocks)
    n_extra = len(rest) - 2 * n_side - 2
    extra = rest[:n_extra]
    side_in = rest[n_extra:n_extra + n_side]
    o_ref = rest[n_extra + n_side]
    side_out = rest[n_extra + n_side + 1:-1]
    wbf_scr = rest[-1]
    jj = pl.program_id(0)
    i = pl.program_id(1)
    n_col = pl.num_programs(0) - 1
    n_chunk = wbf_scr.shape[1] // KC_WS

    step = jj * pl.num_programs(1) + i
    for (first, count), src, dst in zip(side_blocks, side_in, side_out):
        @pl.when(jnp.logical_and(step >= first, step < first + count))
        def _(src=src, dst=dst):
            dst[...] = src[...].astype(BF16)

    @pl.when(jnp.logical_and(jj < n_col, i < n_chunk))
    def _():
        r0 = pl.multiple_of(i * KC_WS, KC_WS)
        wbf_scr[jj % 2, pl.ds(r0, KC_WS), :] = wc_ref[...].astype(BF16)

    @pl.when(jj > 0)
    def _():
        acc = jnp.dot(a_ref[...], wbf_scr[(jj - 1) % 2], preferred_element_type=F32)
        o_ref[...] = epilogue(acc, *extra).astype(o_ref.dtype)


def _ws_matmul(a, w, layer, epilogue, extra=(), side=(), name="ws_matmul"):
    k, n = w.shape[1:]
    n_col = n // BN
    n_row = T_ALL // BM
    n_chunk = k // KC_WS
    assert n_row >= n_chunk and k % KC_WS == 0 and n % BN == 0
    row = lambda jj, i: jnp.where(jj == 0, 0, i)
    col = lambda jj: jnp.maximum(jj - 1, 0)

    side_blocks, side_in, side_out, side_shape = [], [], [], []
    first = 0
    for s in side:
        count = s.shape[1] // SIDE_ROWS
        assert s.shape[1] % SIDE_ROWS == 0 and s.shape[2] == D_MODEL
        blk = lambda jj, i, first=first, count=count: jnp.clip(jj * n_row + i - first, 0, count - 1)
        side_blocks.append((first, count))
        side_in.append(pl.BlockSpec((None, SIDE_ROWS, D_MODEL), lambda jj, i, blk=blk: (layer, blk(jj, i), 0)))
        side_out.append(pl.BlockSpec((SIDE_ROWS, D_MODEL), lambda jj, i, blk=blk: (blk(jj, i), 0)))
        side_shape.append(jax.ShapeDtypeStruct((s.shape[1], D_MODEL), BF16))
        first += count
    assert first <= (n_col + 1) * n_row

    out = pl.pallas_call(
        functools.partial(_ws_matmul_kernel, epilogue, tuple(side_blocks)),
        grid=(n_col + 1, n_row),
        in_specs=[pl.BlockSpec((BM, k), lambda jj, i: (row(jj, i), 0)),
                  pl.BlockSpec((None, KC_WS, BN),
                               lambda jj, i: (layer, jnp.minimum(i, n_chunk - 1), jnp.minimum(jj, n_col - 1)))]
                 + [pl.BlockSpec((None, 1, BN), lambda jj, i: (layer, 0, col(jj)))] * len(extra) + side_in,
        out_specs=[pl.BlockSpec((BM, BN), lambda jj, i: (row(jj, i), col(jj)))] + side_out,
        out_shape=[jax.ShapeDtypeStruct((T_ALL, n), BF16)] + side_shape,
        scratch_shapes=[pltpu.VMEM((2, k, BN), BF16)],
        compiler_params=_params(("arbitrary", "arbitrary"),
                                _vmem_limit([BM * k * 2, KC_WS * BN * 4, BM * BN * 2]
                                            + [SIDE_ROWS * D_MODEL * 6] * len(side),
                                            scratch_bytes=2 * k * BN * 2, temp_bytes=2 * BM * BN * 4)),
        name=name,
    )(a, w, *extra, *side)
    return out if side else out[0]


def _in_proj(h, w_in, layer, side):
    return _ws_matmul(h, w_in, layer, lambda acc: acc, side=side, name="in_proj")


def _kv_proj_kernel(h_ref, w_ref, o_ref):
    o_ref[...] = jnp.dot(h_ref[...], w_ref[...].astype(BF16), preferred_element_type=F32)


def _kv_proj(h, w_in, layer):
    return pl.pallas_call(
        _kv_proj_kernel,
        grid=(2 * KV_DIM // BN_KV, T_CTX // BM),
        in_specs=[pl.BlockSpec((BM, D_MODEL), lambda j, i: (i, 0)),
                  pl.BlockSpec((None, D_MODEL, BN_KV), lambda j, i: (layer, 0, OFF_K // BN_KV + j))],
        out_specs=pl.BlockSpec((BM, BN_KV), lambda j, i: (i, j)),
        out_shape=jax.ShapeDtypeStruct((T_CTX, 2 * KV_DIM), F32),
        compiler_params=_params(("arbitrary", "arbitrary"),
                                _vmem_limit([BM * D_MODEL * 2, D_MODEL * BN_KV * 4, BM * BN_KV * 4],
                                            temp_bytes=D_MODEL * BN_KV * 2 + BM * BN_KV * 4)),
        name="kv_proj",
    )(h, w_in)


def _conv_kernel(cb_ref, cc_ref, cx_ref, w_ref, b_ref, o_ref):
    i = pl.program_id(0)
    seq = jnp.where(i < T_CTX // BM, SEQ, DEC_SEQ)
    u = cc_ref[...].astype(F32) * cx_ref[...].astype(F32)
    pos = lax.broadcasted_iota(jnp.int32, u.shape, 0) & (seq - 1)
    prev = jnp.where(pos == 0, 0.0, pltpu.roll(u, 1, 0))
    nxt = jnp.where(pos == seq - 1, 0.0, pltpu.roll(u, BM - 1, 0))
    w = w_ref[...]
    conv = w[0:1] * prev + w[1:2] * u + w[2:3] * nxt + b_ref[...]
    o_ref[...] = (cb_ref[...].astype(F32) * conv).astype(o_ref.dtype)


def _conv_branch(z, conv_w, conv_b, layer):
    nb = CONV_DIM // BN_CONV
    zspec = lambda off: pl.BlockSpec((BM, BN_CONV), lambda i, j: (i, off // BN_CONV + j))
    return pl.pallas_call(
        _conv_kernel,
        grid=(T_ALL // BM, nb),
        in_specs=[zspec(OFF_CB), zspec(OFF_CC), zspec(OFF_CX),
                  pl.BlockSpec((None, 3, BN_CONV), lambda i, j: (layer, 0, j)),
                  pl.BlockSpec((None, 1, BN_CONV), lambda i, j: (layer, 0, j))],
        out_specs=pl.BlockSpec((BM, BN_CONV), lambda i, j: (i, j)),
        out_shape=jax.ShapeDtypeStruct((T_ALL, CONV_DIM), BF16),
        compiler_params=_params(("arbitrary", "arbitrary"),
                                _vmem_limit([BM * BN_CONV * 2] * 4, temp_bytes=6 * BM * BN_CONV * 4)),
        name="conv_branch",
    )(z, z, z, conv_w, conv_b.reshape(DEPTH, 1, CONV_DIM))


def _sink_column(sink_ref, layer, kvh, rows_per_head):
    row = lax.broadcasted_iota(jnp.int32, (GROUP * rows_per_head, 1), 0)
    col = jnp.full(row.shape, sink_ref[layer, kvh * GROUP], F32)
    for g in range(1, GROUP):
        col = jnp.where(row >= g * rows_per_head, sink_ref[layer, kvh * GROUP + g], col)
    return col


def _stack_heads(q):
    return jnp.concatenate([q[:, g * HEAD_DIM:(g + 1) * HEAD_DIM] for g in range(GROUP)], axis=0)


def _qk(q, k):
    return lax.dot_general(q, k, (((1,), (1,)), ((), ())), preferred_element_type=F32)


def _ctx_attn_kernel(layer, sink_ref, q_ref, k_ref, v_ref, o_ref):
    qw = GROUP * HEAD_DIM
    for kvh in range(N_KV_HEADS):
        q4 = _stack_heads(q_ref[:, kvh * qw:(kvh + 1) * qw])
        t = _qk(q4, k_ref[:, kvh * HEAD_DIM:(kvh + 1) * HEAD_DIM]) * (HEAD_DIM ** -0.5 * LOG2_E)
        sink2 = _sink_column(sink_ref, layer, kvh, SEQ) * LOG2_E
        m = jnp.maximum(jnp.max(t, axis=-1, keepdims=True), sink2)
        e = jnp.exp2(t - m)
        inv = 1.0 / (jnp.sum(e, axis=-1, keepdims=True) + jnp.exp2(sink2 - m))
        o = jnp.dot(e.astype(BF16), v_ref[:, kvh * HEAD_DIM:(kvh + 1) * HEAD_DIM],
                    preferred_element_type=F32) * inv
        for g in range(GROUP):
            c0 = (kvh * GROUP + g) * HEAD_DIM
            o_ref[:, c0:c0 + HEAD_DIM] = o[g * SEQ:(g + 1) * SEQ].astype(o_ref.dtype)


def _ctx_attention(z, attn_sink, layer):
    return pl.pallas_call(
        functools.partial(_ctx_attn_kernel, layer),
        grid=(BATCH,),
        in_specs=[pl.BlockSpec(memory_space=pltpu.SMEM),
                  pl.BlockSpec((SEQ, ATTN_DIM), lambda b: (b, OFF_Q // ATTN_DIM)),
                  pl.BlockSpec((SEQ, KV_DIM), lambda b: (b, OFF_K // KV_DIM)),
                  pl.BlockSpec((SEQ, KV_DIM), lambda b: (b, OFF_V // KV_DIM))],
        out_specs=pl.BlockSpec((SEQ, ATTN_DIM), lambda b: (b, 0)),
        out_shape=jax.ShapeDtypeStruct((T_ALL, ATTN_DIM), BF16),
        compiler_params=_params(("arbitrary",),
                                _vmem_limit([SEQ * ATTN_DIM * 2] * 2 + [SEQ * KV_DIM * 2] * 2,
                                            temp_bytes=N_Q_HEADS * SEQ * SEQ * 4)),
        name="ctx_attention",
    )(attn_sink, z, z, z)


def _rope_tables():
    pos = np.arange(DEC_SEQ)
    row_id = jnp.asarray(pos // GRID_W, F32)
    col_id = jnp.asarray(pos % GRID_W, F32)
    half = HEAD_DIM // 2
    freqs = ROPE_BASE ** (-jnp.arange(0, half, 2, dtype=F32) / half)
    ang_r = row_id[:, None] * freqs[None, :]
    ang_c = col_id[:, None] * freqs[None, :]
    cos = jnp.concatenate([jnp.cos(ang_r), jnp.cos(ang_r), jnp.cos(ang_c), jnp.cos(ang_c)], axis=-1)
    sin = jnp.concatenate([-jnp.sin(ang_r), jnp.sin(ang_r), -jnp.sin(ang_c), jnp.sin(ang_c)], axis=-1)
    return cos, sin


def _rope_kernel(x_ref, cos_ref, sin_ref, o_ref):
    cos = cos_ref[...]
    sin = sin_ref[...]
    quarter = HEAD_DIM // 4
    lane = lax.broadcasted_iota(jnp.int32, cos.shape, 1)
    first = (lane % (2 * quarter)) < quarter
    for g in range(GROUP):
        x = x_ref[:, g * HEAD_DIM:(g + 1) * HEAD_DIM].astype(F32)
        partner = jnp.where(first, pltpu.roll(x, HEAD_DIM - quarter, 1), pltpu.roll(x, quarter, 1))
        o_ref[:, g * HEAD_DIM:(g + 1) * HEAD_DIM] = (x * cos + partner * sin).astype(o_ref.dtype)


def _rope_qk(z, cos, sin):
    w = GROUP * HEAD_DIM
    nb = (ATTN_DIM + KV_DIM) // w
    tab = pl.BlockSpec((DEC_SEQ, HEAD_DIM), lambda b, j: (0, 0))
    return pl.pallas_call(
        _rope_kernel,
        grid=(DEC_BATCH, nb),
        in_specs=[pl.BlockSpec((DEC_SEQ, w), lambda b, j: (T_CTX // DEC_SEQ + b, OFF_Q // w + j)), tab, tab],
        out_specs=pl.BlockSpec((DEC_SEQ, w), lambda b, j: (b, j)),
        out_shape=jax.ShapeDtypeStruct((T_LAT, ATTN_DIM + KV_DIM), BF16),
        compiler_params=_params(("arbitrary", "arbitrary"),
                                _vmem_limit([DEC_SEQ * w * 2] * 2 + [DEC_SEQ * HEAD_DIM * 4] * 2,
                                            temp_bytes=6 * DEC_SEQ * HEAD_DIM * 4)),
        name="rope_qk",
    )(z, cos, sin)


def _lat_attn_kernel(layer, sink_ref, q_ref, k_ref, v_ref, kc_ref, vc_ref, _, o_ref):
    kvh = pl.program_id(1)
    nblk = DEC_SEQ // BLOCK
    scale2 = HEAD_DIM ** -0.5 * LOG2_E
    sink2 = _sink_column(sink_ref, layer, kvh, BLOCK) * LOG2_E
    k_ctx = kc_ref[...].astype(BF16)
    v_ctx = vc_ref[...].astype(BF16)
    band_bias = {}

    def bias(width, q_off):
        if (width, q_off) not in band_bias:
            shape = (GROUP * BLOCK, width)
            qpos = q_off + (lax.broadcasted_iota(jnp.int32, shape, 0) & (BLOCK - 1))
            kpos = lax.broadcasted_iota(jnp.int32, shape, 1)
            band_bias[width, q_off] = jnp.where(jnp.abs(kpos - qpos) <= WINDOW, 0.0, -jnp.inf).astype(F32)
        return band_bias[width, q_off]

    for n in range(nblk):
        lo = max(n - 1, 0) * BLOCK
        hi = min(n + 2, nblk) * BLOCK
        q4 = _stack_heads(q_ref[n * BLOCK:(n + 1) * BLOCK, :])
        t_loc = _qk(q4, k_ref[lo:hi, :]) * scale2 + bias(hi - lo, n * BLOCK - lo)
        t_ctx = _qk(q4, k_ctx) * scale2
        m = jnp.maximum(jnp.maximum(jnp.max(t_loc, axis=-1, keepdims=True),
                                    jnp.max(t_ctx, axis=-1, keepdims=True)), sink2)
        e_loc = jnp.exp2(t_loc - m)
        e_ctx = jnp.exp2(t_ctx - m)
        inv = 1.0 / (jnp.sum(e_loc, axis=-1, keepdims=True) + jnp.sum(e_ctx, axis=-1, keepdims=True)
                     + jnp.exp2(sink2 - m))
        o = (jnp.dot(e_loc.astype(BF16), v_ref[lo:hi, :], preferred_element_type=F32)
             + jnp.dot(e_ctx.astype(BF16), v_ctx, preferred_element_type=F32)) * inv
        for g in range(GROUP):
            o_ref[n * BLOCK:(n + 1) * BLOCK, g * HEAD_DIM:(g + 1) * HEAD_DIM] = (
                o[g * BLOCK:(g + 1) * BLOCK].astype(o_ref.dtype))


def _lat_attention(qk_rot, z, cache_k, cache_v, attn_sink, layer, y_ctx):
    qw = GROUP * HEAD_DIM
    ck = cache_k.reshape(DEC_BATCH, DEPTH, PAST_LEN, KV_DIM)
    cv = cache_v.reshape(DEC_BATCH, DEPTH, PAST_LEN, KV_DIM)
    cspec = pl.BlockSpec((None, None, PAST_LEN, HEAD_DIM), lambda b, h: (b, layer, 0, h))
    return pl.pallas_call(
        functools.partial(_lat_attn_kernel, layer),
        grid=(DEC_BATCH, N_KV_HEADS),
        in_specs=[pl.BlockSpec(memory_space=pltpu.SMEM),
                  pl.BlockSpec((DEC_SEQ, qw), lambda b, h: (b, h)),
                  pl.BlockSpec((DEC_SEQ, HEAD_DIM), lambda b, h: (b, ATTN_DIM // HEAD_DIM + h)),
                  pl.BlockSpec((DEC_SEQ, HEAD_DIM), lambda b, h: (T_CTX // DEC_SEQ + b, OFF_V // HEAD_DIM + h)),
                  cspec, cspec, pl.BlockSpec(memory_space=pl.ANY)],
        out_specs=pl.BlockSpec((DEC_SEQ, qw), lambda b, h: (T_CTX // DEC_SEQ + b, h)),
        out_shape=jax.ShapeDtypeStruct((T_ALL, ATTN_DIM), BF16),
        input_output_aliases={6: 0},
        compiler_params=_params(("arbitrary", "arbitrary"),
                                _vmem_limit([DEC_SEQ * qw * 2] * 2 + [DEC_SEQ * HEAD_DIM * 2] * 2
                                            + [PAST_LEN * HEAD_DIM * 4] * 2,
                                            temp_bytes=DEC_SEQ * GROUP * (3 * BLOCK + PAST_LEN) * 4)),
        name="lat_attention",
    )(attn_sink, qk_rot, qk_rot, z, ck, cv, y_ctx)


def _log_sigmoid(x):
    return jnp.minimum(x, 0.0) - jnp.log(1.0 + jnp.exp(-jnp.abs(x)))


def _head_norm(o, g):
    mu = jnp.mean(o, axis=-1, keepdims=True)
    d = o - mu
    var = jnp.mean(d * d, axis=-1, keepdims=True)
    return d * lax.rsqrt(var + EPS) * g


def _silu(x):
    return x * jax.nn.sigmoid(x)


def _kt_v(k, v):
    return lax.dot_general(k, v, (((0,), (0,)), ((), ())), preferred_element_type=F32)


def _retention_kernel(seq_len, heads, has_state, n_alias, *refs):
    df_ref, db_ref, rq_ref, rk_ref, rv_ref, gf_ref, gb_ref, gn_ref = refs[:8]
    rest = refs[8:]
    if has_state:
        s0f_ref, s0b_ref = rest[:2]
        rest = rest[2:]
    rest = rest[n_alias:]
    if has_state:
        y_ref, of_scr, ob_scr = rest
    else:
        y_ref, sf_ref, sb_ref, of_scr, ob_scr = rest
    n = seq_len // RET_CHUNK
    c = RET_CHUNK
    ii = lax.broadcasted_iota(jnp.int32, (c, c), 0)
    jj = lax.broadcasted_iota(jnp.int32, (c, c), 1)
    row = lax.broadcasted_iota(jnp.int32, (c, RET_DV), 0).astype(F32)

    def direction(hd, lg, rel, q_pow, k_pow, s, order, o_scr):
        cols = slice(hd * RET_DV, (hd + 1) * RET_DV)
        keep = rel >= 0
        k_scale = RET_DK ** -0.5
        dmat = jnp.where(keep, jnp.exp(jnp.where(keep, rel, 0).astype(F32) * lg[:, :c]), 0.0) * k_scale
        q_dec = jnp.exp(q_pow * lg)
        k_dec = jnp.exp(k_pow * lg) * k_scale
        chunk_dec = jnp.exp(c * lg)
        for t in order:
            rows = slice(t * c, (t + 1) * c)
            qc = rq_ref[rows, cols]
            kc = rk_ref[rows, cols]
            vc = rv_ref[rows, cols]
            inner = _qk(qc, kc) * dmat
            o = jnp.dot(inner.astype(BF16), vc, preferred_element_type=F32)
            kv = _kt_v((kc.astype(F32) * k_dec).astype(BF16), vc)
            if s is None:
                s = kv
            else:
                o = o + jnp.dot(qc, s.astype(BF16), preferred_element_type=F32) * q_dec
                s = chunk_dec * s + kv
            o_scr[rows, cols] = o
        return s

    for hd in range(heads):
        lg_f = _log_sigmoid(df_ref[hd])
        lg_b = _log_sigmoid(db_ref[hd])
        s_f = direction(hd, lg_f, ii - jj, row + 1.0, c - 1.0 - row,
                        s0f_ref[hd] if has_state else None, range(n), of_scr)
        s_b = direction(hd, lg_b, jj - ii, c - row, row,
                        s0b_ref[hd] if has_state else None, range(n - 1, -1, -1), ob_scr)
        if not has_state:
            sf_ref[hd] = s_f
            sb_ref[hd] = s_b
        cols = slice(hd * RET_DV, (hd + 1) * RET_DV)
        gn = gn_ref[hd]
        for t in range(n):
            rows = slice(t * c, (t + 1) * c)
            y = (_silu(gf_ref[rows, cols].astype(F32)) * _head_norm(of_scr[rows, cols], gn)
                 + _silu(gb_ref[rows, cols].astype(F32)) * _head_norm(ob_scr[rows, cols], gn))
            y_ref[rows, cols] = y.astype(y_ref.dtype)


def _retention(z, dec_f, dec_b, gn_g, layer, seq_len, row0, n_seq, heads, in_place=(), states=None):
    has_state = states is not None
    blk0 = row0 // seq_len
    w = heads * RET_DV
    assert RET_HEADS % heads == 0 and all(off % w == 0 for off in (OFF_RQ, OFF_RK, OFF_RV, OFF_RGF, OFF_RGB))
    zspec = lambda off: pl.BlockSpec((seq_len, w), lambda b, h: (blk0 + b, off // w + h))
    pspec = pl.BlockSpec((None, heads, 1, RET_DV), lambda b, h: (layer, h, 0, 0))
    sspec = pl.BlockSpec((None, None, heads, RET_DK, RET_DV), lambda b, h: (b, layer, h, 0, 0))
    in_specs = [pspec, pspec, zspec(OFF_RQ), zspec(OFF_RK), zspec(OFF_RV), zspec(OFF_RGF), zspec(OFF_RGB), pspec]
    args = [dec_f, dec_b, z, z, z, z, z, gn_g.reshape(DEPTH, RET_HEADS, 1, RET_DV)]
    if has_state:
        in_specs += [sspec, sspec]
        args += list(states)
    aliases = {}
    for k, prev in enumerate(in_place):
        if prev is not None:
            aliases[len(args)] = k
            in_specs.append(pl.BlockSpec(memory_space=pl.ANY))
            args.append(prev)
    y_spec = pl.BlockSpec((seq_len, w), lambda b, h: (blk0 + b, h))
    y_shape = jax.ShapeDtypeStruct((T_ALL, RET_V_DIM), BF16)
    st_bytes = heads * RET_DK * RET_DV * 4
    if has_state:
        out_specs, out_shape = y_spec, y_shape
    else:
        oshape = jax.ShapeDtypeStruct((n_seq, DEPTH, RET_HEADS, RET_DK, RET_DV), F32)
        out_specs, out_shape = [y_spec, sspec, sspec], [y_shape, oshape, oshape]
    return pl.pallas_call(
        functools.partial(_retention_kernel, seq_len, heads, has_state, len(aliases)),
        grid=(n_seq, RET_HEADS // heads),
        in_specs=in_specs,
        out_specs=out_specs,
        out_shape=out_shape,
        input_output_aliases=aliases,
        scratch_shapes=[pltpu.VMEM((seq_len, w), F32), pltpu.VMEM((seq_len, w), F32)],
        compiler_params=_params(("arbitrary", "arbitrary"),
                                _vmem_limit([seq_len * w * 2] * 6 + [st_bytes] * 2,
                                            scratch_bytes=2 * seq_len * w * 4,
                                            temp_bytes=2 * (seq_len * w * 4 + st_bytes))),
        name="retention",
    )(*args)


def _merge_kernel(yc_ref, ya_ref, yr_ref, wc_ref, wa_ref, wr_ref, ga_ref, gb_ref, gr_ref, o_ref):
    def term(y_ref, w_ref, gate_ref):
        proj = jnp.dot(y_ref[...], w_ref[...], preferred_element_type=F32)
        return jax.nn.sigmoid(gate_ref[...].astype(F32)) * proj

    merged = term(yc_ref, wc_ref, ga_ref) + term(ya_ref, wa_ref, gb_ref) + term(yr_ref, wr_ref, gr_ref)
    o_ref[...] = merged.astype(o_ref.dtype)


def _merge(y_conv, y_attn, y_ret, w_conv_out, w_attn_out, w_ret_out, z, layer):
    bn = BN_MERGE
    yspec = pl.BlockSpec((BM, CONV_DIM), lambda i, j: (i, 0))
    wspec = pl.BlockSpec((None, CONV_DIM, bn), lambda i, j: (layer, 0, j))
    gspec = lambda off: pl.BlockSpec((BM, bn), lambda i, j: (i, off // bn + j))
    return pl.pallas_call(
        _merge_kernel,
        grid=(T_ALL // BM, D_MODEL // bn),
        in_specs=[yspec, yspec, yspec, wspec, wspec, wspec, gspec(OFF_GA), gspec(OFF_GB), gspec(OFF_GR)],
        out_specs=pl.BlockSpec((BM, bn), lambda i, j: (i, j)),
        out_shape=jax.ShapeDtypeStruct((T_ALL, D_MODEL), BF16),
        compiler_params=_params(("arbitrary", "arbitrary"),
                                _vmem_limit([BM * CONV_DIM * 2] * 3 + [CONV_DIM * bn * 2] * 3 + [BM * bn * 2] * 4,
                                            temp_bytes=3 * BM * bn * 4)),
        name="merge",
    )(y_conv, y_attn, y_ret, w_conv_out, w_attn_out, w_ret_out, z, z, z)


def _wo_kernel(m_ref, w_ref, x_ref, g_ref, *rest):
    o_ref = rest[-1]
    acc = jnp.dot(m_ref[...], w_ref[...], preferred_element_type=F32)
    o_ref[...] = x_ref[...] + g_ref[...] * acc


def _out_proj(merged, w_o, x, mod4, layer, row0=0, in_place=None):
    blk0 = row0 // BM
    per = D_MODEL // BN
    extra = [] if in_place is None else [in_place]
    return pl.pallas_call(
        _wo_kernel,
        grid=(x.shape[0] // BM, D_MODEL // BN),
        in_specs=[pl.BlockSpec((BM, D_MODEL), lambda i, j: (blk0 + i, 0)),
                  pl.BlockSpec((D_MODEL, BN), lambda i, j: (0, j)),
                  pl.BlockSpec((BM, BN), lambda i, j: (i, j)),
                  pl.BlockSpec((None, None, 1, BN),
                               lambda i, j: (layer, _cond_row(blk0 + i, BM), 0, 2 * per + j))]
                 + [pl.BlockSpec(memory_space=pl.ANY)] * len(extra),
        out_specs=pl.BlockSpec((BM, BN), lambda i, j: (blk0 + i, j)),
        out_shape=jax.ShapeDtypeStruct((T_ALL, D_MODEL), F32),
        input_output_aliases={4: 0} if extra else {},
        compiler_params=_params(("arbitrary", "arbitrary"),
                                _vmem_limit([BM * D_MODEL * 2, D_MODEL * BN * 2, BM * BN * 4, BM * BN * 4],
                                            temp_bytes=2 * BM * BN * 4)),
        name="out_proj",
    )(merged, w_o, x, mod4, *extra)


def _relu2_epilogue(acc, b_ref):
    r = jnp.maximum(acc + b_ref[...], 0.0)
    return r * r


def _ff1(h, w_ff1, b_ff1, layer):
    return _ws_matmul(h, w_ff1, layer, _relu2_epilogue, extra=(b_ff1.reshape(DEPTH, 1, D_FF),), name="ff1")


def _ff2_kernel(f_ref, w_ref, b_ref, x_ref, g_ref, o_ref, acc_ref):
    k = pl.program_id(2)
    last = pl.num_programs(2) - 1

    def part():
        return jnp.dot(f_ref[...], w_ref[...], preferred_element_type=F32)

    @pl.when(k == 0)
    def _():
        acc_ref[...] = part()

    @pl.when(jnp.logical_and(k > 0, k < last))
    def _():
        acc_ref[...] += part()

    @pl.when(k == last)
    def _():
        o_ref[...] = x_ref[...] + g_ref[...] * ((acc_ref[...] + part()) + b_ref[...])


def _ff2(f, w_ff2, b_ff2, x, mod4, layer):
    bk = BK_FF2
    per = D_MODEL // BN
    assert D_FF // bk >= 2
    return pl.pallas_call(
        _ff2_kernel,
        grid=(T_ALL // BM, D_MODEL // BN, D_FF // bk),
        in_specs=[pl.BlockSpec((BM, bk), lambda i, j, k: (i, k)),
                  pl.BlockSpec((bk, BN), lambda i, j, k: (k, j)),
                  pl.BlockSpec((None, 1, BN), lambda i, j, k: (layer, 0, j)),
                  pl.BlockSpec((BM, BN), lambda i, j, k: (i, j)),
                  pl.BlockSpec((None, None, 1, BN), lambda i, j, k: (layer, _cond_row(i, BM), 0, 5 * per + j))],
        out_specs=pl.BlockSpec((BM, BN), lambda i, j, k: (i, j)),
        out_shape=jax.ShapeDtypeStruct((T_ALL, D_MODEL), F32),
        scratch_shapes=[pltpu.VMEM((BM, BN), F32)],
        compiler_params=_params(("arbitrary", "arbitrary", "arbitrary"),
                                _vmem_limit([BM * bk * 2, bk * BN * 2, BM * BN * 4, BM * BN * 4],
                                            scratch_bytes=BM * BN * 4, temp_bytes=2 * BM * BN * 4)),
        name="ff2",
    )(f, w_ff2, b_ff2.reshape(DEPTH, 1, D_MODEL), x, mod4)


def _ret_param(p):
    return jnp.broadcast_to(p.astype(F32)[:, :, None, None], (DEPTH, RET_HEADS, 1, RET_DV))


def kernel(x_prompt, x_sample, c, cache_k, cache_v, state_ret_f, state_ret_b, c_ctx, w_mod, b_mod, norm1_g, w_in, conv_w, conv_b, attn_sink, ret_decay_f, ret_decay_b, ret_gn_g, w_conv_out, w_attn_out, w_ret_out, w_o, norm2_g, w_ff1, b_ff1, w_ff2, b_ff2, final_g):
    x_parts = ((x_prompt.reshape(T_CTX, D_MODEL), 0), (x_sample.reshape(T_LAT, D_MODEL), T_CTX))
    cond = jnp.concatenate([c_ctx[None, :], c, jnp.zeros((COND_PAD - N_COND, D_MODEL), F32)], axis=0)
    mod4 = _adaln(cond, w_mod, b_mod).reshape(DEPTH, COND_PAD, 1, N_MOD * D_MODEL)

    w_conv_b = w_conv_out.astype(BF16)
    w_attn_b = w_attn_out.astype(BF16)
    w_ret_b = w_ret_out.astype(BF16)
    dec_f = _ret_param(ret_decay_f)
    dec_b = _ret_param(ret_decay_b)
    cos, sin = _rope_tables()

    ks, vs = [], []
    new_states = ()
    for l in range(DEPTH):
        h = None
        for x_rows, row0 in x_parts:
            h = _norm_mod(x_rows, norm1_g, mod4, l, 0, row0, in_place=h)
        z, w_ff2_b, w_o_b = _in_proj(h, w_in, l, side=(w_ff2, w_o))
        kv = _kv_proj(h, w_in, l)
        ks.append(kv[:, :KV_DIM].reshape(BATCH, SEQ, N_KV_HEADS, HEAD_DIM))
        vs.append(kv[:, KV_DIM:].reshape(BATCH, SEQ, N_KV_HEADS, HEAD_DIM))

        y_conv = _conv_branch(z, conv_w, conv_b, l)
        y_attn = _lat_attention(_rope_qk(z, cos, sin), z, cache_k, cache_v, attn_sink, l,
                                _ctx_attention(z, attn_sink, l))
        y_ret, *new_states = _retention(z, dec_f, dec_b, ret_gn_g, l, SEQ, 0, BATCH, RET_HEADS_PER_STEP_CTX,
                                        in_place=(None,) + tuple(new_states) if new_states else ())
        y_ret = _retention(z, dec_f, dec_b, ret_gn_g, l, DEC_SEQ, T_CTX, DEC_BATCH, RET_HEADS_PER_STEP_LAT,
                           in_place=(y_ret,), states=(state_ret_f, state_ret_b))

        merged = _merge(y_conv, y_attn, y_ret, w_conv_b, w_attn_b, w_ret_b, z, l)
        x = None
        for x_rows, row0 in x_parts:
            x = _out_proj(merged, w_o_b, x_rows, mod4, l, row0, in_place=x)
        h = _norm_mod(x, norm2_g, mod4, l, 3)
        f = _ff1(h, w_ff1, b_ff1, l)
        x = _ff2(f, w_ff2_b, b_ff2, x, mod4, l)
        x_parts = ((x, 0),)

    y_prompt = _final_norm(x, final_g, 0, T_CTX).reshape(BATCH, SEQ, D_MODEL)
    y_sample = _final_norm(x, final_g, T_CTX, T_LAT).reshape(DEC_BATCH, DEC_SEQ, D_MODEL)
    return (y_prompt, y_sample, jnp.stack(ks, axis=1), jnp.stack(vs, axis=1), new_states[0], new_states[1])
```
